```python
import jax
import jax.numpy as jnp
from jax import lax
import numpy as np

D_MODEL = 1024
BATCH = 8
SEQ = 2048
DEPTH = 4

MLSTM_HEADS = 4
RET_HEADS = 4
HEAD_DIM = D_MODEL // (MLSTM_HEADS + RET_HEADS)
MLSTM_WIDTH = MLSTM_HEADS * HEAD_DIM
RET_WIDTH = RET_HEADS * HEAD_DIM
CHUNK = 128
QK_CONV = 4
ROPE_BASE = 10000.0
POOL_WINDOWS = (2, 4, 8, 16)
POOL_WIDTH = D_MODEL // 2
POOL_GROUP = POOL_WIDTH // len(POOL_WINDOWS)
CONV_WIDTH = D_MODEL - POOL_WIDTH
DW_KERNEL = 31
D_FF = -(-8 * D_MODEL // (3 * 256)) * 256
EPS = 1e-6
NEG = -1e30
EVEN_SIZES = (MLSTM_WIDTH,) * 4 + (MLSTM_HEADS,) * 2 + (RET_WIDTH,) * 4
EVEN_IN = sum(EVEN_SIZES)
ODD_IN = POOL_WIDTH + 2 * CONV_WIDTH

kernel_name = 'hybrid_mlstm_retention_pool_conv_trunk'


def rmsnorm(x, g):
    xf = x.astype(jnp.float32)
    y = xf * lax.rsqrt(jnp.mean(xf * xf, axis=-1, keepdims=True) + EPS)
    return (y * g.astype(jnp.float32)).astype(x.dtype)


def _normalize(xf):
    xc = xf - jnp.mean(xf, axis=-1, keepdims=True)
    return xc * lax.rsqrt(jnp.mean(xc * xc, axis=-1, keepdims=True) + EPS)


def head_norm(h, g):
    return _normalize(h.astype(jnp.float32)) * g.astype(jnp.float32)


def causal_dwconv(x, w, b):
    width, ch = w.shape
    y = lax.conv_general_dilated(
        x, w[:, None, :].astype(x.dtype), window_strides=(1,),
        padding=[(width - 1, 0)], dimension_numbers=('NWC', 'WIO', 'NWC'),
        feature_group_count=ch)
    return y + b.astype(x.dtype)


def rotary(t):
    S, D = t.shape[1], t.shape[-1]
    inv = ROPE_BASE ** (-jnp.arange(0, D, 2, dtype=jnp.float32) / D)
    ang = jnp.arange(S, dtype=jnp.float32)[:, None] * inv[None, :]
    cos = jnp.cos(ang)[None, :, None, :]
    sin = jnp.sin(ang)[None, :, None, :]
    tf = t.astype(jnp.float32)
    t1, t2 = tf[..., : D // 2], tf[..., D // 2:]
    return jnp.concatenate([t1 * cos - t2 * sin, t1 * sin + t2 * cos], axis=-1)


def _to_chunks(t):
    B, S, H = t.shape[:3]
    t = t.reshape((B, S // CHUNK, CHUNK, H) + t.shape[3:])
    return jnp.moveaxis(t, 3, 1)


def _from_chunks(t):
    B, H, NC, L, D = t.shape
    return jnp.moveaxis(t, 1, 3).reshape(B, NC * L, H, D)


def mlstm_chunkwise(q, k, v, i_pre, f_pre):
    d = q.shape[-1]
    q = _to_chunks(q.astype(jnp.float32))
    k = _to_chunks(k.astype(jnp.float32)) * (d ** -0.5)
    v = _to_chunks(v.astype(jnp.float32))
    ig = _to_chunks(i_pre)
    b = jnp.cumsum(jax.nn.log_sigmoid(_to_chunks(f_pre)), axis=-1)
    causal = jnp.tril(jnp.ones((CHUNK, CHUNK), dtype=bool))
    log_d = jnp.where(causal, b[..., :, None] - b[..., None, :] + ig[..., None, :], NEG)
    m_intra = jnp.max(log_d, axis=-1)
    w_end = b[..., -1:] - b + ig
    m_loc = jnp.max(w_end, axis=-1)
    a_end = jnp.exp(w_end - m_loc[..., None])
    kv_sum = jnp.einsum('bhcl,bhclk,bhclv->bhckv', a_end, k, v)
    n_sum = jnp.einsum('bhcl,bhclk->bhck', a_end, k)
    total = b[..., -1]

    def step(carry, inp):
        C, n, m = carry
        kv_c, n_c, m_c, tot_c = inp
        m_new = jnp.maximum(tot_c + m, m_c)
        s_old = jnp.exp(tot_c + m - m_new)
        s_new = jnp.exp(m_c - m_new)
        C_new = s_old[..., None, None] * C + s_new[..., None, None] * kv_c
        n_new = s_old[..., None] * n + s_new[..., None] * n_c
        return (C_new, n_new, m_new), (C, n, m)

    Bsz, H = q.shape[:2]
    init = (jnp.zeros((Bsz, H, d, d), jnp.float32),
            jnp.zeros((Bsz, H, d), jnp.float32),
            jnp.full((Bsz, H), NEG, jnp.float32))
    xs = tuple(jnp.moveaxis(t, 2, 0) for t in (kv_sum, n_sum, m_loc, total))
    _, (C_prev, n_prev, m_prev) = lax.scan(step, init, xs)
    C_prev = jnp.moveaxis(C_prev, 0, 2)
    n_prev = jnp.moveaxis(n_prev, 0, 2)
    m_prev = jnp.moveaxis(m_prev, 0, 2)

    log_inter = b + m_prev[..., None]
    m = jnp.maximum(m_intra, log_inter)
    s = jnp.einsum('bhcid,bhcjd->bhcij', q, k) * jnp.exp(log_d - m[..., None])
    inter = jnp.exp(log_inter - m)
    num = (jnp.einsum('bhcij,bhcjv->bhciv', s, v)
           + inter[..., None] * jnp.einsum('bhcik,bhckv->bhciv', q, C_prev))
    den = jnp.sum(s, axis=-1) + inter * jnp.einsum('bhcik,bhck->bhci', q, n_prev)
    h = num / jnp.maximum(jnp.abs(den), jnp.exp(-m))[..., None]
    return _from_chunks(h)


def retention_chunkwise(q, k, v):
    d = q.shape[-1]
    n_heads = q.shape[2]
    lg = jnp.log(1.0 - 2.0 ** (-5.0 - jnp.arange(n_heads, dtype=jnp.float32)))
    q = _to_chunks(q.astype(jnp.float32))
    k = _to_chunks(k.astype(jnp.float32)) * (d ** -0.5)
    v = _to_chunks(v.astype(jnp.float32))
    idx = jnp.arange(CHUNK, dtype=jnp.float32)
    rel = idx[:, None] - idx[None, :]
    decay = jnp.where(rel >= 0, jnp.exp(lg[:, None, None] * jnp.maximum(rel, 0.0)), 0.0)
    s = jnp.einsum('bhcid,bhcjd->bhcij', q, k) * decay[None, :, None]
    o = jnp.einsum('bhcij,bhcjv->bhciv', s, v)
    w_end = jnp.exp(lg[:, None] * (CHUNK - 1.0 - idx))
    kv_sum = jnp.einsum('hl,bhclk,bhclv->bhckv', w_end, k, v)
    chunk_decay = jnp.exp(lg * CHUNK)[None, :, None, None]

    def step(R, kv_c):
        return chunk_decay * R + kv_c, R

    init = jnp.zeros(kv_sum.shape[:2] + kv_sum.shape[3:], jnp.float32)
    _, R_prev = lax.scan(step, init, jnp.moveaxis(kv_sum, 2, 0))
    R_prev = jnp.moveaxis(R_prev, 0, 2)
    q_decay = jnp.exp(lg[:, None] * (idx + 1.0))[None, :, None, :, None]
    o = o + jnp.einsum('bhcik,bhckv->bhciv', q, R_prev) * q_decay
    return _from_chunks(o)


def even_mixer(h, w_in, qk_conv_w, qk_conv_b, i_bias, f_bias, mlstm_norm_g, ret_norm_g, w_out):
    Bsz, S, _ = h.shape
    z = h @ w_in
    offs = np.cumsum(EVEN_SIZES)[:-1].tolist()
    mq, mk, mv, mo, mi, mf, rq, rk, rv, rg = jnp.split(z, offs, axis=-1)
    qk = jax.nn.silu(causal_dwconv(jnp.concatenate([mq, mk], axis=-1), qk_conv_w, qk_conv_b))
    mq, mk = jnp.split(qk, 2, axis=-1)

    def heads(t, n):
        return t.reshape(Bsz, S, n, HEAD_DIM)

    i_pre = mi.astype(jnp.float32) + i_bias.astype(jnp.float32)
    f_pre = mf.astype(jnp.float32) + f_bias.astype(jnp.float32)
    h_m = mlstm_chunkwise(heads(mq, MLSTM_HEADS), heads(mk, MLSTM_HEADS),
                          heads(mv, MLSTM_HEADS), i_pre, f_pre)
    h_m = head_norm(jax.nn.sigmoid(heads(mo, MLSTM_HEADS).astype(jnp.float32)) * h_m, mlstm_norm_g)
    h_r = retention_chunkwise(rotary(heads(rq, RET_HEADS)), rotary(heads(rk, RET_HEADS)),
                              heads(rv, RET_HEADS))
    h_r = jax.nn.silu(heads(rg, RET_HEADS).astype(jnp.float32)) * head_norm(h_r, ret_norm_g)
    mixed = jnp.concatenate([h_m.reshape(Bsz, S, MLSTM_WIDTH),
                             h_r.reshape(Bsz, S, RET_WIDTH)], axis=-1).astype(h.dtype)
    return mixed @ w_out


def odd_mixer(h, w_in, pool_w, pool_scale, dw_w, dw_b, conv_norm_g, conv_norm_b, w_out):
    Bsz, S, _ = h.shape
    z = h @ w_in
    u, ga, gb = jnp.split(z, [POOL_WIDTH, POOL_WIDTH + CONV_WIDTH], axis=-1)
    uf = u.astype(jnp.float32).reshape(Bsz, S, len(POOL_WINDOWS), POOL_GROUP)
    csum = jnp.cumsum(uf, axis=1)
    count = jnp.arange(1, S + 1, dtype=jnp.float32)
    groups = []
    for g, w in enumerate(POOL_WINDOWS):
        cs = csum[:, :, g]
        prev = jnp.pad(cs, ((0, 0), (w, 0), (0, 0)))[:, :S]
        mean = (cs - prev) / jnp.minimum(count, float(w))[None, :, None]
        groups.append(mean - uf[:, :, g])
    pooled = jnp.stack(groups, axis=2).astype(h.dtype)
    y_pool = jnp.einsum('bsgi,gio->bsgo', pooled, pool_w).reshape(Bsz, S, POOL_WIDTH) * pool_scale
    y = ga * jax.nn.sigmoid(gb)
    y = causal_dwconv(y, dw_w, dw_b)
    yf = _normalize(y.astype(jnp.float32)) * conv_norm_g + conv_norm_b
    y_conv = jax.nn.silu(yf).astype(h.dtype)
    mixed = jnp.concatenate([y_pool.astype(h.dtype), y_conv], axis=-1)
    return mixed @ w_out


def swiglu(h, w_gate, w_up, w_down):
    return (jax.nn.silu(h @ w_gate) * (h @ w_up)) @ w_down


def setup_inputs(seed: int = 0) -> dict:
    key = jax.random.key(seed)
    ks = jax.random.split(key, 24)
    n_even = (DEPTH + 1) // 2
    n_odd = DEPTH // 2

    def nrm(k, shape, scale):
        return jax.random.normal(k, shape, jnp.float32) * scale

    return {
        'x': nrm(ks[0], (BATCH, SEQ, D_MODEL), 1.0),
        'norm_mix_g': 1.0 + nrm(ks[1], (DEPTH, D_MODEL), 0.02),
        'norm_ffn_g': 1.0 + nrm(ks[2], (DEPTH, D_MODEL), 0.02),
        'final_norm_g': 1.0 + nrm(ks[3], (D_MODEL,), 0.02),
        'ev_w_in': nrm(ks[4], (n_even, D_MODEL, EVEN_IN), D_MODEL ** -0.5),
        'ev_qk_conv_w': nrm(ks[5], (n_even, QK_CONV, 2 * MLSTM_WIDTH), QK_CONV ** -0.5),
        'ev_qk_conv_b': nrm(ks[6], (n_even, 2 * MLSTM_WIDTH), 0.02),
        'ev_i_bias': nrm(ks[7], (n_even, MLSTM_HEADS), 0.1),
        'ev_f_bias': jnp.linspace(3.0, 6.0, MLSTM_HEADS, dtype=jnp.float32)[None, :]
                     + nrm(ks[8], (n_even, MLSTM_HEADS), 0.1),
        'ev_mlstm_norm_g': 1.0 + nrm(ks[9], (n_even, MLSTM_HEADS, HEAD_DIM), 0.02),
        'ev_ret_norm_g': 1.0 + nrm(ks[10], (n_even, RET_HEADS, HEAD_DIM), 0.02),
        'ev_w_out': nrm(ks[11], (n_even, MLSTM_WIDTH + RET_WIDTH, D_MODEL), (MLSTM_WIDTH + RET_WIDTH) ** -0.5),
        'od_w_in': nrm(ks[12], (n_odd, D_MODEL, ODD_IN), D_MODEL ** -0.5),
        'od_pool_w': nrm(ks[13], (n_odd, len(POOL_WINDOWS), POOL_GROUP, POOL_GROUP), POOL_GROUP ** -0.5),
        'od_pool_scale': 1.0 + nrm(ks[14], (n_odd, POOL_WIDTH), 0.1),
        'od_dw_w': nrm(ks[15], (n_odd, DW_KERNEL, CONV_WIDTH), DW_KERNEL ** -0.5),
        'od_dw_b': nrm(ks[16], (n_odd, CONV_WIDTH), 0.02),
        'od_conv_norm_g': 1.0 + nrm(ks[17], (n_odd, CONV_WIDTH), 0.02),
        'od_conv_norm_b': nrm(ks[18], (n_odd, CONV_WIDTH), 0.02),
        'od_w_out': nrm(ks[19], (n_odd, POOL_WIDTH + CONV_WIDTH, D_MODEL), (POOL_WIDTH + CONV_WIDTH) ** -0.5),
        'ffn_w_gate': nrm(ks[20], (DEPTH, D_MODEL, D_FF), D_MODEL ** -0.5),
        'ffn_w_up': nrm(ks[21], (DEPTH, D_MODEL, D_FF), D_MODEL ** -0.5),
        'ffn_w_down': nrm(ks[22], (DEPTH, D_FF, D_MODEL), D_FF ** -0.5),
    }


def reference(x, norm_mix_g, norm_ffn_g, final_norm_g, ev_w_in, ev_qk_conv_w, ev_qk_conv_b,
              ev_i_bias, ev_f_bias, ev_mlstm_norm_g, ev_ret_norm_g, ev_w_out, od_w_in,
              od_pool_w, od_pool_scale, od_dw_w, od_dw_b, od_conv_norm_g, od_conv_norm_b,
              od_w_out, ffn_w_gate, ffn_w_up, ffn_w_down):
    for layer in range(DEPTH):
        j = layer // 2
        h = rmsnorm(x, norm_mix_g[layer])
        if layer % 2 == 0:
            y = even_mixer(h, ev_w_in[j], ev_qk_conv_w[j], ev_qk_conv_b[j], ev_i_bias[j],
                           ev_f_bias[j], ev_mlstm_norm_g[j], ev_ret_norm_g[j], ev_w_out[j])
        else:
            y = odd_mixer(h, od_w_in[j], od_pool_w[j], od_pool_scale[j], od_dw_w[j], od_dw_b[j],
                          od_conv_norm_g[j], od_conv_norm_b[j], od_w_out[j])
        x = x + y.astype(x.dtype)
        h = rmsnorm(x, norm_ffn_g[layer])
        x = x + swiglu(h, ffn_w_gate[layer], ffn_w_up[layer], ffn_w_down[layer]).astype(x.dtype)
    return rmsnorm(x, final_norm_g)
```

```python
import functools
import math

import numpy as np
import jax
import jax.numpy as jnp
from jax import lax
from jax.experimental import pallas as pl
from jax.experimental.pallas import tpu as pltpu

F32 = jnp.float32
BF16 = jnp.bfloat16

EPS = 1e-6
NEG = -1e30
CHUNK = 128
HEAD_DIM = 128
N_HEADS = 4
QK_CONV = 4
ROPE_BASE = 10000.0
POOL_WINDOWS = (2, 4, 8, 16)
POOL_GROUP = 128
DW_KERNEL = 31

SEQ_TILE = 512
FFN_TILE = 512
FF_CHUNK = 256
QK_HALO = 8
POOL_HALO = 16
DW_HALO = 32
VMEM_LIMIT_BYTES = 56 * 1024 * 1024


def _rmsnorm(x, g):
    ms = jnp.mean(x * x, axis=-1, keepdims=True)
    return (x * lax.rsqrt(ms + EPS)) * g


def _normalize(x):
    xc = x - jnp.mean(x, axis=-1, keepdims=True)
    return xc * lax.rsqrt(jnp.mean(xc * xc, axis=-1, keepdims=True) + EPS)


def _sigmoid(x):
    return 1.0 / (1.0 + jnp.exp(-x))


def _silu(x):
    return x * _sigmoid(x)


def _log_sigmoid(x):
    return jnp.minimum(x, 0.0) - jnp.log(1.0 + jnp.exp(-jnp.abs(x)))


def _dot(a, b):
    return jnp.dot(a.astype(BF16), b.astype(BF16), preferred_element_type=F32)


def _dot_nt(a, b):
    return lax.dot_general(a.astype(BF16), b.astype(BF16), (((1,), (1,)), ((), ())),
                           preferred_element_type=F32)


def _dot_tn(a, b):
    return lax.dot_general(a.astype(BF16), b.astype(BF16), (((0,), (0,)), ((), ())),
                           preferred_element_type=F32)


def _ffn_kernel(x_ref, g_ref, wgu_ref, wd_ref, fg_ref, o_ref, *, n_chunks, apply_final_norm):
    x = x_ref[...]
    h = _rmsnorm(x, g_ref[...]).astype(BF16)
    acc = x
    for c in range(n_chunks):
        gu = jnp.dot(h, wgu_ref[c], preferred_element_type=F32)
        gate = gu[:, :FF_CHUNK]
        up = gu[:, FF_CHUNK:]
        a = (_silu(gate) * up).astype(BF16)
        acc = acc + jnp.dot(a, wd_ref[c], preferred_element_type=F32)
    if apply_final_norm:
        acc = _rmsnorm(acc, fg_ref[...])
    o_ref[...] = acc


def _ffn(x2, norm_g, w_gate, w_up, w_down, final_g, apply_final_norm):
    n_tok, d = x2.shape
    d_ff = w_gate.shape[1]
    n_chunks = d_ff // FF_CHUNK
    assert n_chunks * FF_CHUNK == d_ff and n_tok % FFN_TILE == 0
    wg = w_gate.astype(BF16).reshape(d, n_chunks, FF_CHUNK)
    wu = w_up.astype(BF16).reshape(d, n_chunks, FF_CHUNK)
    wgu = jnp.transpose(jnp.concatenate([wg, wu], axis=-1), (1, 0, 2))
    wd = w_down.astype(BF16).reshape(n_chunks, FF_CHUNK, d)
    const2 = lambda i: (0, 0)
    const3 = lambda i: (0, 0, 0)
    return pl.pallas_call(
        functools.partial(_ffn_kernel, n_chunks=n_chunks, apply_final_norm=apply_final_norm),
        grid=(n_tok // FFN_TILE,),
        in_specs=[
            pl.BlockSpec((FFN_TILE, d), lambda i: (i, 0)),
            pl.BlockSpec((1, d), const2),
            pl.BlockSpec((n_chunks, d, 2 * FF_CHUNK), const3),
            pl.BlockSpec((n_chunks, FF_CHUNK, d), const3),
            pl.BlockSpec((1, d), const2),
        ],
        out_specs=pl.BlockSpec((FFN_TILE, d), lambda i: (i, 0)),
        out_shape=jax.ShapeDtypeStruct((n_tok, d), F32),
        compiler_params=pltpu.CompilerParams(
            dimension_semantics=("arbitrary",), vmem_limit_bytes=VMEM_LIMIT_BYTES),
        name="ffn",
    )(x2, norm_g.reshape(1, d), wgu, wd, final_g.reshape(1, d))


def _odd_kernel(x_ref, g_ref, win_ref, poolw_ref, pscale_ref, dww_ref, dwb_ref, cng_ref, cnb_ref,
                wout_ref, o_ref, ubuf, ybuf, *, pool_width):
    i = pl.program_id(1)
    ts = x_ref.shape[0]

    @pl.when(i == 0)
    def _():
        ubuf[0:POOL_HALO, :] = jnp.zeros((POOL_HALO, ubuf.shape[1]), F32)
        ybuf[0:DW_HALO, :] = jnp.zeros((DW_HALO, ybuf.shape[1]), F32)

    x = x_ref[...]
    h = _rmsnorm(x, g_ref[...]).astype(BF16)
    z = jnp.dot(h, win_ref[...], preferred_element_type=F32)
    conv_width = (z.shape[1] - pool_width) // 2
    ubuf[POOL_HALO:POOL_HALO + ts, :] = z[:, :pool_width]
    ga = z[:, pool_width:pool_width + conv_width]
    gb = z[:, pool_width + conv_width:]
    ybuf[DW_HALO:DW_HALO + ts, :] = ga * _sigmoid(gb)

    pos = (i * ts + lax.broadcasted_iota(jnp.int32, (ts, 1), 0)).astype(F32) + 1.0
    out = x
    for g, w in enumerate(POOL_WINDOWS):
        cols = slice(g * POOL_GROUP, (g + 1) * POOL_GROUP)
        cur = ubuf[POOL_HALO:POOL_HALO + ts, cols]
        wsum = cur
        for k in range(1, w):
            wsum = wsum + ubuf[POOL_HALO - k:POOL_HALO - k + ts, cols]
        pooled = wsum / jnp.minimum(pos, float(w)) - cur
        y_pool = _dot(pooled, poolw_ref[g]) * pscale_ref[:, cols]
        out = out + _dot(y_pool, wout_ref[cols, :])

    conv = jnp.zeros((ts, conv_width), F32) + dwb_ref[...]
    for k in range(DW_KERNEL):
        start = DW_HALO - (DW_KERNEL - 1) + k
        conv = conv + dww_ref[k:k + 1, :] * ybuf[start:start + ts, :]
    yf = _normalize(conv) * cng_ref[...] + cnb_ref[...]
    y_conv = _silu(yf)
    out = out + _dot(y_conv, wout_ref[pool_width:, :])
    o_ref[...] = out

    ubuf[0:POOL_HALO, :] = ubuf[ts:ts + POOL_HALO, :]
    ybuf[0:DW_HALO, :] = ybuf[ts:ts + DW_HALO, :]


def _odd_mixer(x2, batch, seq, norm_g, w_in, pool_w, pool_scale, dw_w, dw_b, cn_g, cn_b, w_out):
    n_tok, d = x2.shape
    pool_width = pool_w.shape[0] * pool_w.shape[1]
    conv_width = dw_w.shape[1]
    ts = SEQ_TILE
    nt = seq // ts
    assert nt * ts == seq
    row = lambda b, i: (b * nt + i, 0)
    c2 = lambda b, i: (0, 0)
    c3 = lambda b, i: (0, 0, 0)
    return pl.pallas_call(
        functools.partial(_odd_kernel, pool_width=pool_width),
        grid=(batch, nt),
        in_specs=[
            pl.BlockSpec((ts, d), row),
            pl.BlockSpec((1, d), c2),
            pl.BlockSpec(w_in.shape, c2),
            pl.BlockSpec(pool_w.shape, c3),
            pl.BlockSpec((1, pool_width), c2),
            pl.BlockSpec(dw_w.shape, c2),
            pl.BlockSpec((1, conv_width), c2),
            pl.BlockSpec((1, conv_width), c2),
            pl.BlockSpec((1, conv_width), c2),
            pl.BlockSpec(w_out.shape, c2),
        ],
        out_specs=pl.BlockSpec((ts, d), row),
        out_shape=jax.ShapeDtypeStruct((n_tok, d), F32),
        scratch_shapes=[
            pltpu.VMEM((ts + POOL_HALO, pool_width), F32),
            pltpu.VMEM((ts + DW_HALO, conv_width), F32),
        ],
        compiler_params=pltpu.CompilerParams(
            dimension_semantics=("arbitrary", "arbitrary"), vmem_limit_bytes=VMEM_LIMIT_BYTES),
        name="odd_mixer",
    )(x2, norm_g.reshape(1, d), w_in.astype(BF16), pool_w.astype(BF16),
      pool_scale.reshape(1, pool_width), dw_w, dw_b.reshape(1, conv_width),
      cn_g.reshape(1, conv_width), cn_b.reshape(1, conv_width), w_out.astype(BF16))


def _chunk_cumsum_rows(x, n):
    idx = lax.broadcasted_iota(jnp.int32, x.shape, 0) % n
    s = 1
    while s < n:
        x = x + jnp.where(idx >= s, pltpu.roll(x, s, axis=0), 0.0)
        s *= 2
    return x


def _chunk_cumsum_lanes(x, n):
    idx = lax.broadcasted_iota(jnp.int32, x.shape, 1) % n
    s = 1
    while s < n:
        x = x + jnp.where(idx >= s, pltpu.roll(x, s, axis=1), 0.0)
        s *= 2
    return x


def _head_norm(x, g):
    return _normalize(x) * g


def _even_kernel(x_ref, g_ref, wbig_ref, wgc_ref, wgr_ref, gbc_ref, gbr_ref, cw_ref, cb_ref,
                 cos_ref, sin_ref, mng_ref, rng_ref, wout_ref, o_ref,
                 zbuf, qkbuf, mixbuf, c_state, n_state, m_state, r_state, *, ret_log_decay):
    i = pl.program_id(1)
    ts = x_ref.shape[0]
    width = N_HEADS * HEAD_DIM
    scale = HEAD_DIM ** -0.5

    @pl.when(i == 0)
    def _():
        zbuf[0:QK_HALO, :] = jnp.zeros((QK_HALO, zbuf.shape[1]), F32)
        c_state[...] = jnp.zeros(c_state.shape, F32)
        n_state[...] = jnp.zeros(n_state.shape, F32)
        m_state[...] = jnp.full(m_state.shape, NEG, F32)
        r_state[...] = jnp.zeros(r_state.shape, F32)

    x = x_ref[...]
    h = _rmsnorm(x, g_ref[...]).astype(BF16)
    zbuf[QK_HALO:QK_HALO + ts, :] = jnp.dot(h, wbig_ref[...], preferred_element_type=F32)

    gate_c = jnp.dot(h, wgc_ref[...], preferred_element_type=F32) + gbc_ref[...]
    gate_r = _dot_nt(wgr_ref[...], h) + gbr_ref[...]
    ig_c = gate_c[:, 0:N_HEADS]
    b_c = _chunk_cumsum_rows(_log_sigmoid(gate_c), CHUNK)[:, N_HEADS:2 * N_HEADS]
    ig_r = gate_r[0:N_HEADS, :]
    b_r = _chunk_cumsum_lanes(_log_sigmoid(gate_r), CHUNK)[N_HEADS:2 * N_HEADS, :]

    conv = jnp.zeros((ts, 2 * width), F32) + cb_ref[...]
    for k in range(QK_CONV):
        start = QK_HALO - (QK_CONV - 1) + k
        conv = conv + cw_ref[k:k + 1, :] * zbuf[start:start + ts, 0:2 * width]
    qkbuf[...] = _silu(conv)

    row_i = lax.broadcasted_iota(jnp.int32, (CHUNK, CHUNK), 0)
    col_j = lax.broadcasted_iota(jnp.int32, (CHUNK, CHUNK), 1)
    causal = col_j <= row_i
    rel = (row_i - col_j).astype(F32)
    pos_c = lax.broadcasted_iota(jnp.int32, (CHUNK, 1), 0).astype(F32)

    n_chunks = ts // CHUNK
    for hd in range(N_HEADS):
        cols = slice(hd * HEAD_DIM, (hd + 1) * HEAD_DIM)
        c_prev = c_state[hd]
        n_prev = n_state[hd]
        m_prev = m_state[hd][:, 0:1]
        for c in range(n_chunks):
            r0 = c * CHUNK
            rows = slice(r0, r0 + CHUNK)
            zrows = slice(QK_HALO + r0, QK_HALO + r0 + CHUNK)
            q = qkbuf[rows, cols]
            k = qkbuf[rows, width + hd * HEAD_DIM: width + (hd + 1) * HEAD_DIM] * scale
            v = zbuf[zrows, 2 * width + hd * HEAD_DIM: 2 * width + (hd + 1) * HEAD_DIM]
            og = zbuf[zrows, 3 * width + hd * HEAD_DIM: 3 * width + (hd + 1) * HEAD_DIM]
            bc = b_c[rows, hd:hd + 1]
            ic = ig_c[rows, hd:hd + 1]
            br = b_r[hd:hd + 1, rows]
            ir = ig_r[hd:hd + 1, rows]
            log_d = jnp.where(causal, bc - br + ir, NEG)
            m_intra = jnp.max(log_d, axis=1, keepdims=True)
            total = bc[CHUNK - 1:CHUNK, :]
            w_end = total - bc + ic
            m_loc = jnp.max(w_end, axis=0, keepdims=True)
            ak = jnp.exp(w_end - m_loc) * k
            kv_sum = _dot_tn(ak, v)
            n_sum = jnp.sum(ak, axis=0, keepdims=True)

            log_inter = bc + m_prev
            m = jnp.maximum(m_intra, log_inter)
            s = _dot_nt(q, k) * jnp.exp(log_d - m)
            inter = jnp.exp(log_inter - m)
            num = _dot(s, v) + inter * _dot(q, c_prev)
            den = (jnp.sum(s, axis=1, keepdims=True)
                   + inter * jnp.sum(q * n_prev, axis=1, keepdims=True))
            hm = num / jnp.maximum(jnp.abs(den), jnp.exp(-m))
            hm = _head_norm(_sigmoid(og) * hm, mng_ref[hd:hd + 1, :])
            mixbuf[rows, cols] = hm.astype(BF16)

            m_new = jnp.maximum(total + m_prev, m_loc)
            s_old = jnp.exp(total + m_prev - m_new)
            s_new = jnp.exp(m_loc - m_new)
            c_prev = s_old * c_prev + s_new * kv_sum
            n_prev = s_old * n_prev + s_new * n_sum
            m_prev = m_new
        c_state[hd] = c_prev
        n_state[hd] = n_prev
        m_state[hd] = jnp.broadcast_to(m_prev, (1, HEAD_DIM))

        lg = ret_log_decay[hd]
        decay = jnp.where(causal, jnp.exp(lg * jnp.maximum(rel, 0.0)), 0.0)
        w_end_r = jnp.exp(lg * (CHUNK - 1.0 - pos_c))
        q_decay = jnp.exp(lg * (pos_c + 1.0))
        chunk_decay = math.exp(lg * CHUNK)
        r_prev = r_state[hd]
        for c in range(n_chunks):
            r0 = c * CHUNK
            rows = slice(r0, r0 + CHUNK)
            zrows = slice(QK_HALO + r0, QK_HALO + r0 + CHUNK)
            cos = cos_ref[rows, :]
            sin = sin_ref[rows, :]
            rq = zbuf[zrows, 4 * width + hd * HEAD_DIM: 4 * width + (hd + 1) * HEAD_DIM]
            rk = zbuf[zrows, 5 * width + hd * HEAD_DIM: 5 * width + (hd + 1) * HEAD_DIM]
            rv = zbuf[zrows, 6 * width + hd * HEAD_DIM: 6 * width + (hd + 1) * HEAD_DIM]
            rg = zbuf[zrows, 7 * width + hd * HEAD_DIM: 7 * width + (hd + 1) * HEAD_DIM]
            q = rq * cos + pltpu.roll(rq, HEAD_DIM // 2, axis=1) * sin
            k = (rk * cos + pltpu.roll(rk, HEAD_DIM // 2, axis=1) * sin) * scale
            s = _dot_nt(q, k) * decay
            o = _dot(s, rv) + _dot(q, r_prev) * q_decay
            r_prev = chunk_decay * r_prev + _dot_tn(w_end_r * k, rv)
            hr = _silu(rg) * _head_norm(o, rng_ref[hd:hd + 1, :])
            mixbuf[rows, width + hd * HEAD_DIM: width + (hd + 1) * HEAD_DIM] = hr.astype(BF16)
        r_state[hd] = r_prev

    o_ref[...] = x + jnp.dot(mixbuf[...], wout_ref[...], preferred_element_type=F32)
    zbuf[0:QK_HALO, :] = zbuf[ts:ts + QK_HALO, :]


def _retention_log_decays():
    h = np.arange(N_HEADS, dtype=np.float32)
    lg = np.log(np.float32(1.0) - np.float32(2.0) ** (np.float32(-5.0) - h)).astype(np.float32)
    return tuple(float(v) for v in lg)


def _rotary_tables(seq):
    half = HEAD_DIM // 2
    inv = ROPE_BASE ** (-jnp.arange(0, HEAD_DIM, 2, dtype=F32) / HEAD_DIM)
    ang = jnp.arange(seq, dtype=F32)[:, None] * inv[None, :]
    cos = jnp.cos(ang)
    sin = jnp.sin(ang)
    return jnp.concatenate([cos, cos], axis=-1), jnp.concatenate([-sin, sin], axis=-1)


def _even_mixer(x2, batch, seq, norm_g, w_in, qk_conv_w, qk_conv_b, i_bias, f_bias,
                mlstm_norm_g, ret_norm_g, w_out):
    n_tok, d = x2.shape
    width = N_HEADS * HEAD_DIM
    ts = SEQ_TILE
    nt = seq // ts
    assert nt * ts == seq and ts % CHUNK == 0
    g0 = 4 * width
    g1 = g0 + 2 * N_HEADS
    w_big = jnp.concatenate([w_in[:, :g0], w_in[:, g1:]], axis=1).astype(BF16)
    w_gate = w_in[:, g0:g1].astype(BF16)
    w_gate_c = jnp.pad(w_gate, ((0, 0), (0, HEAD_DIM - 2 * N_HEADS)))
    w_gate_r = w_gate.T
    gate_b = jnp.concatenate([i_bias, f_bias]).astype(F32)
    gate_b_c = jnp.pad(gate_b, (0, HEAD_DIM - 2 * N_HEADS)).reshape(1, HEAD_DIM)
    gate_b_r = gate_b.reshape(2 * N_HEADS, 1)
    cos_t, sin_t = _rotary_tables(seq)

    row = lambda b, i: (b * nt + i, 0)
    seq_row = lambda b, i: (i, 0)
    c2 = lambda b, i: (0, 0)
    return pl.pallas_call(
        functools.partial(_even_kernel, ret_log_decay=_retention_log_decays()),
        grid=(batch, nt),
        in_specs=[
            pl.BlockSpec((ts, d), row),
            pl.BlockSpec((1, d), c2),
            pl.BlockSpec(w_big.shape, c2),
            pl.BlockSpec(w_gate_c.shape, c2),
            pl.BlockSpec(w_gate_r.shape, c2),
            pl.BlockSpec(gate_b_c.shape, c2),
            pl.BlockSpec(gate_b_r.shape, c2),
            pl.BlockSpec(qk_conv_w.shape, c2),
            pl.BlockSpec((1, 2 * width), c2),
            pl.BlockSpec((ts, HEAD_DIM), seq_row),
            pl.BlockSpec((ts, HEAD_DIM), seq_row),
            pl.BlockSpec(mlstm_norm_g.shape, c2),
            pl.BlockSpec(ret_norm_g.shape, c2),
            pl.BlockSpec(w_out.shape, c2),
        ],
        out_specs=pl.BlockSpec((ts, d), row),
        out_shape=jax.ShapeDtypeStruct((n_tok, d), F32),
        scratch_shapes=[
            pltpu.VMEM((ts + QK_HALO, 8 * width), F32),
            pltpu.VMEM((ts, 2 * width), F32),
            pltpu.VMEM((ts, 2 * width), BF16),
            pltpu.VMEM((N_HEADS, HEAD_DIM, HEAD_DIM), F32),
            pltpu.VMEM((N_HEADS, 1, HEAD_DIM), F32),
            pltpu.VMEM((N_HEADS, 1, HEAD_DIM), F32),
            pltpu.VMEM((N_HEADS, HEAD_DIM, HEAD_DIM), F32),
        ],
        compiler_params=pltpu.CompilerParams(
            dimension_semantics=("arbitrary", "arbitrary"), vmem_limit_bytes=VMEM_LIMIT_BYTES),
        name="even_mixer",
    )(x2, norm_g.reshape(1, d), w_big, w_gate_c, w_gate_r, gate_b_c, gate_b_r,
      qk_conv_w, qk_conv_b.reshape(1, 2 * width), cos_t, sin_t,
      mlstm_norm_g, ret_norm_g, w_out.astype(BF16))


def kernel(x, norm_mix_g, norm_ffn_g, final_norm_g, ev_w_in, ev_qk_conv_w, ev_qk_conv_b, ev_i_bias, ev_f_bias, ev_mlstm_norm_g, ev_ret_norm_g, ev_w_out, od_w_in, od_pool_w, od_pool_scale, od_dw_w, od_dw_b, od_conv_norm_g, od_conv_norm_b, od_w_out, ffn_w_gate, ffn_w_up, ffn_w_down):
    batch, seq, d = x.shape
    depth = norm_mix_g.shape[0]
    x2 = x.reshape(batch * seq, d)
    for layer in range(depth):
        j = layer // 2
        if layer % 2 == 0:
            x2 = _even_mixer(x2, batch, seq, norm_mix_g[layer], ev_w_in[j], ev_qk_conv_w[j],
                             ev_qk_conv_b[j], ev_i_bias[j], ev_f_bias[j], ev_mlstm_norm_g[j],
                             ev_ret_norm_g[j], ev_w_out[j])
        else:
            x2 = _odd_mixer(x2, batch, seq, norm_mix_g[layer], od_w_in[j], od_pool_w[j],
                            od_pool_scale[j], od_dw_w[j], od_dw_b[j], od_conv_norm_g[j],
                            od_conv_norm_b[j], od_w_out[j])
        x2 = _ffn(x2, norm_ffn_g[layer], ffn_w_gate[layer], ffn_w_up[layer], ffn_w_down[layer],
                  final_norm_g, apply_final_norm=(layer == depth - 1))
    return x2.reshape(batch, seq, d)
```

```python
import functools
import math

import numpy as np
import jax
import jax.numpy as jnp
from jax import lax
from jax.experimental import pallas as pl
from jax.experimental.pallas import tpu as pltpu

F32 = jnp.float32
BF16 = jnp.bfloat16

EPS = 1e-6
NEG = -1e30
CHUNK = 128
HEAD_DIM = 128
N_HEADS = 4
QK_CONV = 4
ROPE_BASE = 10000.0
POOL_WINDOWS = (2, 4, 8, 16)
POOL_GROUP = 128
DW_KERNEL = 31

SEQ_TILE = 512
FFN_TILE = 512
FF_CHUNK = 256
QK_HALO = 8
POOL_HALO = 16
DW_HALO = 32
DW_ROWS = 64
VMEM_LIMIT_BYTES = 56 * 1024 * 1024


def _rmsnorm(x, g):
    ms = jnp.mean(x * x, axis=-1, keepdims=True)
    return (x * lax.rsqrt(ms + EPS)) * g


def _normalize(x):
    xc = x - jnp.mean(x, axis=-1, keepdims=True)
    return xc * lax.rsqrt(jnp.mean(xc * xc, axis=-1, keepdims=True) + EPS)


def _sigmoid(x):
    return 1.0 / (1.0 + jnp.exp(-x))


def _silu(x):
    return x * _sigmoid(x)


def _log_sigmoid(x):
    return jnp.minimum(x, 0.0) - jnp.log(1.0 + jnp.exp(-jnp.abs(x)))


def _dot(a, b):
    return jnp.dot(a.astype(BF16), b.astype(BF16), preferred_element_type=F32)


def _dot_nt(a, b):
    return lax.dot_general(a.astype(BF16), b.astype(BF16), (((1,), (1,)), ((), ())),
                           preferred_element_type=F32)


def _dot_tn(a, b):
    return lax.dot_general(a.astype(BF16), b.astype(BF16), (((0,), (0,)), ((), ())),
                           preferred_element_type=F32)


def _ffn_kernel(x_ref, g_ref, wg_ref, wu_ref, wd_ref, fg_ref, o_ref, *, apply_final_norm):
    x = x_ref[...]
    h = _rmsnorm(x, g_ref[...]).astype(BF16)
    acc = x
    d_ff = wg_ref.shape[1]
    for c0 in range(0, d_ff, FF_CHUNK):
        cols = slice(c0, c0 + FF_CHUNK)
        gate = jnp.dot(h, wg_ref[:, cols], preferred_element_type=F32)
        up = jnp.dot(h, wu_ref[:, cols], preferred_element_type=F32)
        a = (_silu(gate) * up).astype(BF16)
        acc = acc + jnp.dot(a, wd_ref[cols, :], preferred_element_type=F32)
    if apply_final_norm:
        acc = _rmsnorm(acc, fg_ref[...])
    o_ref[...] = acc


def _ffn(x2, layer, norm_g, w_gate, w_up, w_down, final_g, apply_final_norm):
    n_tok, d = x2.shape
    d_ff = w_gate.shape[2]
    assert d_ff % FF_CHUNK == 0 and n_tok % FFN_TILE == 0
    const2 = lambda i: (0, 0)
    pick = lambda i: (layer, 0, 0)
    return pl.pallas_call(
        functools.partial(_ffn_kernel, apply_final_norm=apply_final_norm),
        grid=(n_tok // FFN_TILE,),
        in_specs=[
            pl.BlockSpec((FFN_TILE, d), lambda i: (i, 0)),
            pl.BlockSpec((1, d), const2),
            pl.BlockSpec((None, d, d_ff), pick),
            pl.BlockSpec((None, d, d_ff), pick),
            pl.BlockSpec((None, d_ff, d), pick),
            pl.BlockSpec((1, d), const2),
        ],
        out_specs=pl.BlockSpec((FFN_TILE, d), lambda i: (i, 0)),
        out_shape=jax.ShapeDtypeStruct((n_tok, d), F32),
        compiler_params=pltpu.CompilerParams(
            dimension_semantics=("arbitrary",), vmem_limit_bytes=VMEM_LIMIT_BYTES),
        name="ffn",
    )(x2, norm_g.reshape(1, d), w_gate, w_up, w_down, final_g.reshape(1, d))


def _odd_kernel(x_ref, g_ref, win_ref, poolw_ref, pscale_ref, dww_ref, dwb_ref, cng_ref, cnb_ref,
                wout_ref, o_ref, ubuf, ybuf, yrot, cbuf, *, pool_width):
    i = pl.program_id(1)
    ts = x_ref.shape[0]

    @pl.when(i == 0)
    def _():
        ubuf[0:POOL_HALO, :] = jnp.zeros((POOL_HALO, ubuf.shape[1]), F32)
        ybuf[0:DW_HALO, :] = jnp.zeros((DW_HALO, ybuf.shape[1]), F32)

    x = x_ref[...]
    h = _rmsnorm(x, g_ref[...]).astype(BF16)
    z = jnp.dot(h, win_ref[...], preferred_element_type=F32)
    conv_width = (z.shape[1] - pool_width) // 2
    ubuf[POOL_HALO:POOL_HALO + ts, :] = z[:, :pool_width]
    ga = z[:, pool_width:pool_width + conv_width]
    gb = z[:, pool_width + conv_width:]
    ybuf[DW_HALO:DW_HALO + ts, :] = ga * _sigmoid(gb)

    pos = (i * ts + lax.broadcasted_iota(jnp.int32, (ts, 1), 0)).astype(F32) + 1.0
    out = x
    for g, w in enumerate(POOL_WINDOWS):
        cols = slice(g * POOL_GROUP, (g + 1) * POOL_GROUP)
        wsum = ubuf[:, cols]
        shift = 1
        while shift < w:
            wsum = wsum + pltpu.roll(wsum, shift, axis=0)
            shift *= 2
        cur = ubuf[POOL_HALO:POOL_HALO + ts, cols]
        pooled = wsum[POOL_HALO:, :] / jnp.minimum(pos, float(w)) - cur
        y_pool = _dot(pooled, poolw_ref[g]) * pscale_ref[:, cols]
        out = out + _dot(y_pool, wout_ref[cols, :])

    rot_rows = yrot.shape[1]
    for r in range(1, 8):
        yrot[r - 1] = ybuf[r:r + rot_rows, :]
    for r0 in range(0, ts, DW_ROWS):
        for c0 in range(0, conv_width, 128):
            cols = slice(c0, c0 + 128)
            acc = jnp.broadcast_to(dwb_ref[:, cols], (DW_ROWS, 128))
            for k in range(DW_KERNEL):
                start = DW_HALO - (DW_KERNEL - 1) + k + r0
                src = ybuf if start % 8 == 0 else yrot.at[start % 8 - 1]
                aligned = start - start % 8
                acc = acc + dww_ref[k:k + 1, cols] * src[aligned:aligned + DW_ROWS, cols]
            cbuf[r0:r0 + DW_ROWS, cols] = acc
    yf = _normalize(cbuf[...]) * cng_ref[...] + cnb_ref[...]
    y_conv = _silu(yf)
    out = out + _dot(y_conv, wout_ref[pool_width:, :])
    o_ref[...] = out

    ubuf[0:POOL_HALO, :] = ubuf[ts:ts + POOL_HALO, :]
    ybuf[0:DW_HALO, :] = ybuf[ts:ts + DW_HALO, :]


def _odd_mixer(x2, batch, seq, norm_g, w_in, pool_w, pool_scale, dw_w, dw_b, cn_g, cn_b, w_out):
    n_tok, d = x2.shape
    pool_width = pool_w.shape[0] * pool_w.shape[1]
    conv_width = dw_w.shape[1]
    ts = SEQ_TILE
    nt = seq // ts
    assert nt * ts == seq
    row = lambda b, i: (b * nt + i, 0)
    c2 = lambda b, i: (0, 0)
    c3 = lambda b, i: (0, 0, 0)
    return pl.pallas_call(
        functools.partial(_odd_kernel, pool_width=pool_width),
        grid=(batch, nt),
        in_specs=[
            pl.BlockSpec((ts, d), row),
            pl.BlockSpec((1, d), c2),
            pl.BlockSpec(w_in.shape, c2),
            pl.BlockSpec(pool_w.shape, c3),
            pl.BlockSpec((1, pool_width), c2),
            pl.BlockSpec(dw_w.shape, c2),
            pl.BlockSpec((1, conv_width), c2),
            pl.BlockSpec((1, conv_width), c2),
            pl.BlockSpec((1, conv_width), c2),
            pl.BlockSpec(w_out.shape, c2),
        ],
        out_specs=pl.BlockSpec((ts, d), row),
        out_shape=jax.ShapeDtypeStruct((n_tok, d), F32),
        scratch_shapes=[
            pltpu.VMEM((ts + POOL_HALO, pool_width), F32),
            pltpu.VMEM((ts + DW_HALO, conv_width), F32),
            pltpu.VMEM((7, ts + DW_HALO - 8, conv_width), F32),
            pltpu.VMEM((ts, conv_width), F32),
        ],
        compiler_params=pltpu.CompilerParams(
            dimension_semantics=("arbitrary", "arbitrary"), vmem_limit_bytes=VMEM_LIMIT_BYTES),
        name="odd_mixer",
    )(x2, norm_g.reshape(1, d), w_in.astype(BF16), pool_w.astype(BF16),
      pool_scale.reshape(1, pool_width), dw_w, dw_b.reshape(1, conv_width),
      cn_g.reshape(1, conv_width), cn_b.reshape(1, conv_width), w_out.astype(BF16))


def _chunk_cumsum_rows(x, n):
    idx = lax.broadcasted_iota(jnp.int32, x.shape, 0) % n
    s = 1
    while s < n:
        x = x + jnp.where(idx >= s, pltpu.roll(x, s, axis=0), 0.0)
        s *= 2
    return x


def _chunk_cumsum_lanes(x, n):
    idx = lax.broadcasted_iota(jnp.int32, x.shape, 1) % n
    s = 1
    while s < n:
        x = x + jnp.where(idx >= s, pltpu.roll(x, s, axis=1), 0.0)
        s *= 2
    return x


def _head_norm(x, g):
    return _normalize(x) * g


def _even_kernel(x_ref, g_ref, wbig_ref, wgc_ref, wgr_ref, gbc_ref, gbr_ref, cw_ref, cb_ref,
                 cos_ref, sin_ref, mng_ref, rng_ref, wout_ref, o_ref,
                 zbuf, qkbuf, mixbuf, c_state, n_state, m_state, r_state, *, ret_log_decay):
    i = pl.program_id(1)
    ts = x_ref.shape[0]
    width = N_HEADS * HEAD_DIM
    scale = HEAD_DIM ** -0.5

    @pl.when(i == 0)
    def _():
        zbuf[0:QK_HALO, :] = jnp.zeros((QK_HALO, zbuf.shape[1]), F32)
        c_state[...] = jnp.zeros(c_state.shape, F32)
        n_state[...] = jnp.zeros(n_state.shape, F32)
        m_state[...] = jnp.full(m_state.shape, NEG, F32)
        r_state[...] = jnp.zeros(r_state.shape, F32)

    x = x_ref[...]
    h = _rmsnorm(x, g_ref[...]).astype(BF16)
    zbuf[QK_HALO:QK_HALO + ts, :] = jnp.dot(h, wbig_ref[...], preferred_element_type=F32)

    gate_c = jnp.dot(h, wgc_ref[...], preferred_element_type=F32) + gbc_ref[...]
    gate_r = _dot_nt(wgr_ref[...], h) + gbr_ref[...]
    ig_c = gate_c[:, 0:N_HEADS]
    b_c = _chunk_cumsum_rows(_log_sigmoid(gate_c), CHUNK)[:, N_HEADS:2 * N_HEADS]
    ig_r = gate_r[0:N_HEADS, :]
    b_r = _chunk_cumsum_lanes(_log_sigmoid(gate_r), CHUNK)[N_HEADS:2 * N_HEADS, :]

    conv = jnp.zeros((ts, 2 * width), F32) + cb_ref[...]
    for k in range(QK_CONV):
        start = QK_HALO - (QK_CONV - 1) + k
        conv = conv + cw_ref[k:k + 1, :] * zbuf[start:start + ts, 0:2 * width]
    qkbuf[...] = _silu(conv)

    row_i = lax.broadcasted_iota(jnp.int32, (CHUNK, CHUNK), 0)
    col_j = lax.broadcasted_iota(jnp.int32, (CHUNK, CHUNK), 1)
    causal = col_j <= row_i
    rel = (row_i - col_j).astype(F32)
    pos_c = lax.broadcasted_iota(jnp.int32, (CHUNK, 1), 0).astype(F32)

    n_chunks = ts // CHUNK
    for hd in range(N_HEADS):
        cols = slice(hd * HEAD_DIM, (hd + 1) * HEAD_DIM)
        c_prev = c_state[hd]
        n_prev = n_state[hd]
        m_prev = m_state[hd][:, 0:1]
        for c in range(n_chunks):
            r0 = c * CHUNK
            rows = slice(r0, r0 + CHUNK)
            zrows = slice(QK_HALO + r0, QK_HALO + r0 + CHUNK)
            q = qkbuf[rows, cols]
            k = qkbuf[rows, width + hd * HEAD_DIM: width + (hd + 1) * HEAD_DIM] * scale
            v = zbuf[zrows, 2 * width + hd * HEAD_DIM: 2 * width + (hd + 1) * HEAD_DIM]
            og = zbuf[zrows, 3 * width + hd * HEAD_DIM: 3 * width + (hd + 1) * HEAD_DIM]
            bc = b_c[rows, hd:hd + 1]
            ic = ig_c[rows, hd:hd + 1]
            br = b_r[hd:hd + 1, rows]
            ir = ig_r[hd:hd + 1, rows]
            log_d = jnp.where(causal, bc - br + ir, NEG)
            m_intra = jnp.max(log_d, axis=1, keepdims=True)
            total = bc[CHUNK - 1:CHUNK, :]
            w_end = total - bc + ic
            m_loc = jnp.max(w_end, axis=0, keepdims=True)
            ak = jnp.exp(w_end - m_loc) * k
            kv_sum = _dot_tn(ak, v)
            n_sum = jnp.sum(ak, axis=0, keepdims=True)

            log_inter = bc + m_prev
            m = jnp.maximum(m_intra, log_inter)
            s = _dot_nt(q, k) * jnp.exp(log_d - m)
            inter = jnp.exp(log_inter - m)
            num = _dot(s, v) + inter * _dot(q, c_prev)
            den = (jnp.sum(s, axis=1, keepdims=True)
                   + inter * jnp.sum(q * n_prev, axis=1, keepdims=True))
            hm = num / jnp.maximum(jnp.abs(den), jnp.exp(-m))
            hm = _head_norm(_sigmoid(og) * hm, mng_ref[hd:hd + 1, :])
            mixbuf[rows, cols] = hm.astype(BF16)

            m_new = jnp.maximum(total + m_prev, m_loc)
            s_old = jnp.exp(total + m_prev - m_new)
            s_new = jnp.exp(m_loc - m_new)
            c_prev = s_old * c_prev + s_new * kv_sum
            n_prev = s_old * n_prev + s_new * n_sum
            m_prev = m_new
        c_state[hd] = c_prev
        n_state[hd] = n_prev
        m_state[hd] = jnp.broadcast_to(m_prev, (1, HEAD_DIM))

        lg = ret_log_decay[hd]
        decay = jnp.where(causal, jnp.exp(lg * jnp.maximum(rel, 0.0)), 0.0)
        w_end_r = jnp.exp(lg * (CHUNK - 1.0 - pos_c))
        q_decay = jnp.exp(lg * (pos_c + 1.0))
        chunk_decay = math.exp(lg * CHUNK)
        r_prev = r_state[hd]
        for c in range(n_chunks):
            r0 = c * CHUNK
            rows = slice(r0, r0 + CHUNK)
            zrows = slice(QK_HALO + r0, QK_HALO + r0 + CHUNK)
            cos = cos_ref[rows, :]
            sin = sin_ref[rows, :]
            rq = zbuf[zrows, 4 * width + hd * HEAD_DIM: 4 * width + (hd + 1) * HEAD_DIM]
            rk = zbuf[zrows, 5 * width + hd * HEAD_DIM: 5 * width + (hd + 1) * HEAD_DIM]
            rv = zbuf[zrows, 6 * width + hd * HEAD_DIM: 6 * width + (hd + 1) * HEAD_DIM]
            rg = zbuf[zrows, 7 * width + hd * HEAD_DIM: 7 * width + (hd + 1) * HEAD_DIM]
            q = rq * cos + pltpu.roll(rq, HEAD_DIM // 2, axis=1) * sin
            k = (rk * cos + pltpu.roll(rk, HEAD_DIM // 2, axis=1) * sin) * scale
            s = _dot_nt(q, k) * decay
            o = _dot(s, rv) + _dot(q, r_prev) * q_decay
            r_prev = chunk_decay * r_prev + _dot_tn(w_end_r * k, rv)
            hr = _silu(rg) * _head_norm(o, rng_ref[hd:hd + 1, :])
            mixbuf[rows, width + hd * HEAD_DIM: width + (hd + 1) * HEAD_DIM] = hr.astype(BF16)
        r_state[hd] = r_prev

    o_ref[...] = x + jnp.dot(mixbuf[...], wout_ref[...], preferred_element_type=F32)
    zbuf[0:QK_HALO, :] = zbuf[ts:ts + QK_HALO, :]


def _retention_log_decays():
    h = np.arange(N_HEADS, dtype=np.float32)
    lg = np.log(np.float32(1.0) - np.float32(2.0) ** (np.float32(-5.0) - h)).astype(np.float32)
    return tuple(float(v) for v in lg)


def _rotary_tables(seq):
    half = HEAD_DIM // 2
    inv = ROPE_BASE ** (-jnp.arange(0, HEAD_DIM, 2, dtype=F32) / HEAD_DIM)
    ang = jnp.arange(seq, dtype=F32)[:, None] * inv[None, :]
    cos = jnp.cos(ang)
    sin = jnp.sin(ang)
    return jnp.concatenate([cos, cos], axis=-1), jnp.concatenate([-sin, sin], axis=-1)


def _even_mixer(x2, batch, seq, norm_g, w_in, qk_conv_w, qk_conv_b, i_bias, f_bias,
                mlstm_norm_g, ret_norm_g, w_out):
    n_tok, d = x2.shape
    width = N_HEADS * HEAD_DIM
    ts = SEQ_TILE
    nt = seq // ts
    assert nt * ts == seq and ts % CHUNK == 0
    g0 = 4 * width
    g1 = g0 + 2 * N_HEADS
    w_big = jnp.concatenate([w_in[:, :g0], w_in[:, g1:]], axis=1).astype(BF16)
    w_gate = w_in[:, g0:g1].astype(BF16)
    w_gate_c = jnp.pad(w_gate, ((0, 0), (0, HEAD_DIM - 2 * N_HEADS)))
    w_gate_r = w_gate.T
    gate_b = jnp.concatenate([i_bias, f_bias]).astype(F32)
    gate_b_c = jnp.pad(gate_b, (0, HEAD_DIM - 2 * N_HEADS)).reshape(1, HEAD_DIM)
    gate_b_r = gate_b.reshape(2 * N_HEADS, 1)
    cos_t, sin_t = _rotary_tables(seq)

    row = lambda b, i: (b * nt + i, 0)
    seq_row = lambda b, i: (i, 0)
    c2 = lambda b, i: (0, 0)
    return pl.pallas_call(
        functools.partial(_even_kernel, ret_log_decay=_retention_log_decays()),
        grid=(batch, nt),
        in_specs=[
            pl.BlockSpec((ts, d), row),
            pl.BlockSpec((1, d), c2),
            pl.BlockSpec(w_big.shape, c2),
            pl.BlockSpec(w_gate_c.shape, c2),
            pl.BlockSpec(w_gate_r.shape, c2),
            pl.BlockSpec(gate_b_c.shape, c2),
            pl.BlockSpec(gate_b_r.shape, c2),
            pl.BlockSpec(qk_conv_w.shape, c2),
            pl.BlockSpec((1, 2 * width), c2),
            pl.BlockSpec((ts, HEAD_DIM), seq_row),
            pl.BlockSpec((ts, HEAD_DIM), seq_row),
            pl.BlockSpec(mlstm_norm_g.shape, c2),
            pl.BlockSpec(ret_norm_g.shape, c2),
            pl.BlockSpec(w_out.shape, c2),
        ],
        out_specs=pl.BlockSpec((ts, d), row),
        out_shape=jax.ShapeDtypeStruct((n_tok, d), F32),
        scratch_shapes=[
            pltpu.VMEM((ts + QK_HALO, 8 * width), F32),
            pltpu.VMEM((ts, 2 * width), F32),
            pltpu.VMEM((ts, 2 * width), BF16),
            pltpu.VMEM((N_HEADS, HEAD_DIM, HEAD_DIM), F32),
            pltpu.VMEM((N_HEADS, 1, HEAD_DIM), F32),
            pltpu.VMEM((N_HEADS, 1, HEAD_DIM), F32),
            pltpu.VMEM((N_HEADS, HEAD_DIM, HEAD_DIM), F32),
        ],
        compiler_params=pltpu.CompilerParams(
            dimension_semantics=("arbitrary", "arbitrary"), vmem_limit_bytes=VMEM_LIMIT_BYTES),
        name="even_mixer",
    )(x2, norm_g.reshape(1, d), w_big, w_gate_c, w_gate_r, gate_b_c, gate_b_r,
      qk_conv_w, qk_conv_b.reshape(1, 2 * width), cos_t, sin_t,
      mlstm_norm_g, ret_norm_g, w_out.astype(BF16))


def kernel(x, norm_mix_g, norm_ffn_g, final_norm_g, ev_w_in, ev_qk_conv_w, ev_qk_conv_b, ev_i_bias, ev_f_bias, ev_mlstm_norm_g, ev_ret_norm_g, ev_w_out, od_w_in, od_pool_w, od_pool_scale, od_dw_w, od_dw_b, od_conv_norm_g, od_conv_norm_b, od_w_out, ffn_w_gate, ffn_w_up, ffn_w_down):
    batch, seq, d = x.shape
    depth = norm_mix_g.shape[0]
    x2 = x.reshape(batch * seq, d)
    w_gate, w_up, w_down = (w.astype(BF16) for w in (ffn_w_gate, ffn_w_up, ffn_w_down))
    for layer in range(depth):
        j = layer // 2
        if layer % 2 == 0:
            x2 = _even_mixer(x2, batch, seq, norm_mix_g[layer], ev_w_in[j], ev_qk_conv_w[j],
                             ev_qk_conv_b[j], ev_i_bias[j], ev_f_bias[j], ev_mlstm_norm_g[j],
                             ev_ret_norm_g[j], ev_w_out[j])
        else:
            x2 = _odd_mixer(x2, batch, seq, norm_mix_g[layer], od_w_in[j], od_pool_w[j],
                            od_pool_scale[j], od_dw_w[j], od_dw_b[j], od_conv_norm_g[j],
                            od_conv_norm_b[j], od_w_out[j])
        x2 = _ffn(x2, layer, norm_ffn_g[layer], w_gate, w_up, w_down,
                  final_norm_g, apply_final_norm=(layer == depth - 1))
    return x2.reshape(batch, seq, d)
```

```python
import functools
import math

import numpy as np
import jax
import jax.numpy as jnp
from jax import lax
from jax.experimental import pallas as pl
from jax.experimental.pallas import tpu as pltpu

F32 = jnp.float32
BF16 = jnp.bfloat16

EPS = 1e-6
NEG = -1e30
CHUNK = 128
HEAD_DIM = 128
N_HEADS = 4
QK_CONV = 4
ROPE_BASE = 10000.0
POOL_WINDOWS = (2, 4, 8, 16)
POOL_GROUP = 128
DW_KERNEL = 31

SEQ_TILE = 512
FFN_TILE = 512
FF_CHUNK = 256
QK_HALO = 8
POOL_HALO = 16
DW_HALO = 32
DW_ROWS = 64
VMEM_LIMIT_BYTES = 56 * 1024 * 1024


def _rmsnorm(x, g):
    ms = jnp.mean(x * x, axis=-1, keepdims=True)
    return (x * lax.rsqrt(ms + EPS)) * g


def _normalize(x):
    xc = x - jnp.mean(x, axis=-1, keepdims=True)
    return xc * lax.rsqrt(jnp.mean(xc * xc, axis=-1, keepdims=True) + EPS)


def _sigmoid(x):
    return 1.0 / (1.0 + jnp.exp(-x))


def _silu(x):
    return x * _sigmoid(x)


def _log_sigmoid(x):
    return jnp.minimum(x, 0.0) - jnp.log(1.0 + jnp.exp(-jnp.abs(x)))


def _dot(a, b):
    return jnp.dot(a.astype(BF16), b.astype(BF16), preferred_element_type=F32)


def _dot_nt(a, b):
    return lax.dot_general(a.astype(BF16), b.astype(BF16), (((1,), (1,)), ((), ())),
                           preferred_element_type=F32)


def _dot_tn(a, b):
    return lax.dot_general(a.astype(BF16), b.astype(BF16), (((0,), (0,)), ((), ())),
                           preferred_element_type=F32)


def _ffn_kernel(x_ref, g_ref, wg_ref, wu_ref, wd_ref, fg_ref, o_ref, *, apply_final_norm):
    x = x_ref[...]
    h = _rmsnorm(x, g_ref[...]).astype(BF16)
    acc = x
    d_ff = wg_ref.shape[1]
    for c0 in range(0, d_ff, FF_CHUNK):
        cols = slice(c0, c0 + FF_CHUNK)
        gate = jnp.dot(h, wg_ref[:, cols].astype(BF16), preferred_element_type=F32)
        up = jnp.dot(h, wu_ref[:, cols].astype(BF16), preferred_element_type=F32)
        a = (_silu(gate) * up).astype(BF16)
        acc = acc + jnp.dot(a, wd_ref[cols, :].astype(BF16), preferred_element_type=F32)
    if apply_final_norm:
        acc = _rmsnorm(acc, fg_ref[...])
    o_ref[...] = acc


def _ffn(x2, layer, norm_g, w_gate, w_up, w_down, final_g, apply_final_norm):
    n_tok, d = x2.shape
    d_ff = w_gate.shape[2]
    assert d_ff % FF_CHUNK == 0 and n_tok % FFN_TILE == 0
    const2 = lambda i: (0, 0)
    pick = lambda i: (layer, 0, 0)
    return pl.pallas_call(
        functools.partial(_ffn_kernel, apply_final_norm=apply_final_norm),
        grid=(n_tok // FFN_TILE,),
        in_specs=[
            pl.BlockSpec((FFN_TILE, d), lambda i: (i, 0)),
            pl.BlockSpec((1, d), const2),
            pl.BlockSpec((None, d, d_ff), pick, pipeline_mode=pl.Buffered(1)),
            pl.BlockSpec((None, d, d_ff), pick, pipeline_mode=pl.Buffered(1)),
            pl.BlockSpec((None, d_ff, d), pick, pipeline_mode=pl.Buffered(1)),
            pl.BlockSpec((1, d), const2),
        ],
        out_specs=pl.BlockSpec((FFN_TILE, d), lambda i: (i, 0)),
        out_shape=jax.ShapeDtypeStruct((n_tok, d), F32),
        compiler_params=pltpu.CompilerParams(
            dimension_semantics=("arbitrary",), vmem_limit_bytes=VMEM_LIMIT_BYTES),
        name="ffn",
    )(x2, norm_g.reshape(1, d), w_gate, w_up, w_down, final_g.reshape(1, d))


def _odd_kernel(x_ref, g_ref, win_ref, poolw_ref, pscale_ref, dww_ref, dwb_ref, cng_ref, cnb_ref,
                wout_ref, o_ref, ubuf, ybuf, yrot, cbuf, *, pool_width):
    i = pl.program_id(1)
    ts = x_ref.shape[0]

    @pl.when(i == 0)
    def _():
        ubuf[0:POOL_HALO, :] = jnp.zeros((POOL_HALO, ubuf.shape[1]), F32)
        ybuf[0:DW_HALO, :] = jnp.zeros((DW_HALO, ybuf.shape[1]), F32)

    x = x_ref[...]
    h = _rmsnorm(x, g_ref[...]).astype(BF16)
    z = jnp.dot(h, win_ref[...], preferred_element_type=F32)
    conv_width = (z.shape[1] - pool_width) // 2
    ubuf[POOL_HALO:POOL_HALO + ts, :] = z[:, :pool_width]
    ga = z[:, pool_width:pool_width + conv_width]
    gb = z[:, pool_width + conv_width:]
    ybuf[DW_HALO:DW_HALO + ts, :] = ga * _sigmoid(gb)

    pos = (i * ts + lax.broadcasted_iota(jnp.int32, (ts, 1), 0)).astype(F32) + 1.0
    out = x
    for g, w in enumerate(POOL_WINDOWS):
        cols = slice(g * POOL_GROUP, (g + 1) * POOL_GROUP)
        wsum = ubuf[:, cols]
        shift = 1
        while shift < w:
            wsum = wsum + pltpu.roll(wsum, shift, axis=0)
            shift *= 2
        cur = ubuf[POOL_HALO:POOL_HALO + ts, cols]
        pooled = wsum[POOL_HALO:, :] / jnp.minimum(pos, float(w)) - cur
        y_pool = _dot(pooled, poolw_ref[g]) * pscale_ref[:, cols]
        out = out + _dot(y_pool, wout_ref[cols, :])

    rot_rows = yrot.shape[1]
    for r in range(1, 8):
        yrot[r - 1] = ybuf[r:r + rot_rows, :]
    for r0 in range(0, ts, DW_ROWS):
        for c0 in range(0, conv_width, 128):
            cols = slice(c0, c0 + 128)
            acc = jnp.broadcast_to(dwb_ref[:, cols], (DW_ROWS, 128))
            for k in range(DW_KERNEL):
                start = DW_HALO - (DW_KERNEL - 1) + k + r0
                src = ybuf if start % 8 == 0 else yrot.at[start % 8 - 1]
                aligned = start - start % 8
                acc = acc + dww_ref[k:k + 1, cols] * src[aligned:aligned + DW_ROWS, cols]
            cbuf[r0:r0 + DW_ROWS, cols] = acc
    yf = _normalize(cbuf[...]) * cng_ref[...] + cnb_ref[...]
    y_conv = _silu(yf)
    out = out + _dot(y_conv, wout_ref[pool_width:, :])
    o_ref[...] = out

    ubuf[0:POOL_HALO, :] = ubuf[ts:ts + POOL_HALO, :]
    ybuf[0:DW_HALO, :] = ybuf[ts:ts + DW_HALO, :]


def _odd_mixer(x2, batch, seq, norm_g, w_in, pool_w, pool_scale, dw_w, dw_b, cn_g, cn_b, w_out):
    n_tok, d = x2.shape
    pool_width = pool_w.shape[0] * pool_w.shape[1]
    conv_width = dw_w.shape[1]
    ts = SEQ_TILE
    nt = seq // ts
    assert nt * ts == seq
    row = lambda b, i: (b * nt + i, 0)
    c2 = lambda b, i: (0, 0)
    c3 = lambda b, i: (0, 0, 0)
    return pl.pallas_call(
        functools.partial(_odd_kernel, pool_width=pool_width),
        grid=(batch, nt),
        in_specs=[
            pl.BlockSpec((ts, d), row),
            pl.BlockSpec((1, d), c2),
            pl.BlockSpec(w_in.shape, c2),
            pl.BlockSpec(pool_w.shape, c3),
            pl.BlockSpec((1, pool_width), c2),
            pl.BlockSpec(dw_w.shape, c2),
            pl.BlockSpec((1, conv_width), c2),
            pl.BlockSpec((1, conv_width), c2),
            pl.BlockSpec((1, conv_width), c2),
            pl.BlockSpec(w_out.shape, c2),
        ],
        out_specs=pl.BlockSpec((ts, d), row),
        out_shape=jax.ShapeDtypeStruct((n_tok, d), F32),
        scratch_shapes=[
            pltpu.VMEM((ts + POOL_HALO, pool_width), F32),
            pltpu.VMEM((ts + DW_HALO, conv_width), F32),
            pltpu.VMEM((7, ts + DW_HALO - 8, conv_width), F32),
            pltpu.VMEM((ts, conv_width), F32),
        ],
        compiler_params=pltpu.CompilerParams(
            dimension_semantics=("arbitrary", "arbitrary"), vmem_limit_bytes=VMEM_LIMIT_BYTES),
        name="odd_mixer",
    )(x2, norm_g.reshape(1, d), w_in.astype(BF16), pool_w.astype(BF16),
      pool_scale.reshape(1, pool_width), dw_w, dw_b.reshape(1, conv_width),
      cn_g.reshape(1, conv_width), cn_b.reshape(1, conv_width), w_out.astype(BF16))


PAIR_WIDTH = 2 * HEAD_DIM
N_PAIRS = N_HEADS // 2
PAIR_COLS = 4 * PAIR_WIDTH


def _chunk_cumsum_rows(x, n):
    idx = lax.broadcasted_iota(jnp.int32, x.shape, 0) % n
    s = 1
    while s < n:
        x = x + jnp.where(idx >= s, pltpu.roll(x, s, axis=0), 0.0)
        s *= 2
    return x


def _chunk_cumsum_lanes(x, n):
    idx = lax.broadcasted_iota(jnp.int32, x.shape, 1) % n
    s = 1
    while s < n:
        x = x + jnp.where(idx >= s, pltpu.roll(x, s, axis=1), 0.0)
        s *= 2
    return x


def _halves(a2):
    return a2[:, :HEAD_DIM], a2[:, HEAD_DIM:]


def _pair(a0, a1):
    return jnp.concatenate([a0, a1], axis=1)


def _block_diag(a2):
    a0, a1 = _halves(a2)
    z = jnp.zeros_like(a0)
    return jnp.concatenate([_pair(a0, z), _pair(z, a1)], axis=0)


def _pair_head_norm(x2, g2):
    x0, x1 = _halves(x2)
    return _pair(_normalize(x0), _normalize(x1)) * g2


def _rotary_pair(t2, cos, sin):
    t0, t1 = _halves(t2)
    return _pair(t0 * cos + pltpu.roll(t0, HEAD_DIM // 2, axis=1) * sin,
                 t1 * cos + pltpu.roll(t1, HEAD_DIM // 2, axis=1) * sin)


def _even_kernel(x_ref, g_ref, wbig_ref, wgc_ref, wgr_ref, gbc_ref, gbr_ref, cw_ref, cb_ref,
                 cos_ref, sin_ref, mng_ref, rng_ref, wout_ref, o_ref,
                 zbuf, qkbuf, mixbuf, c_state, n_state, m_state, r_state, *, ret_log_decay):
    i = pl.program_id(1)
    ts = x_ref.shape[0]
    scale = HEAD_DIM ** -0.5
    n_chunks = ts // CHUNK

    @pl.when(i == 0)
    def _():
        zbuf[0:QK_HALO, :] = jnp.zeros((QK_HALO, zbuf.shape[1]), F32)
        c_state[...] = jnp.zeros(c_state.shape, F32)
        n_state[...] = jnp.zeros(n_state.shape, F32)
        m_state[...] = jnp.full(m_state.shape, NEG, F32)
        r_state[...] = jnp.zeros(r_state.shape, F32)

    x = x_ref[...]
    h = _rmsnorm(x, g_ref[...]).astype(BF16)
    for pp in range(2 * N_PAIRS):
        cols = slice(pp * PAIR_COLS, (pp + 1) * PAIR_COLS)
        zbuf[QK_HALO:QK_HALO + ts, cols] = jnp.dot(h, wbig_ref[:, cols],
                                                   preferred_element_type=F32)

    gate_c = jnp.dot(h, wgc_ref[...], preferred_element_type=F32) + gbc_ref[...]
    gate_r = _dot_nt(wgr_ref[...], h) + gbr_ref[...]
    ig_c = gate_c[:, 0:N_HEADS]
    b_c = _chunk_cumsum_rows(_log_sigmoid(gate_c), CHUNK)[:, N_HEADS:2 * N_HEADS]
    ig_r = gate_r[0:N_HEADS, :]
    b_r = _chunk_cumsum_lanes(_log_sigmoid(gate_r), CHUNK)[N_HEADS:2 * N_HEADS, :]

    row_i = lax.broadcasted_iota(jnp.int32, (CHUNK, CHUNK), 0)
    col_j = lax.broadcasted_iota(jnp.int32, (CHUNK, CHUNK), 1)
    causal = col_j <= row_i
    rel = (row_i - col_j).astype(F32)
    pos_c = lax.broadcasted_iota(jnp.int32, (CHUNK, 1), 0).astype(F32)
    blk_r = lax.broadcasted_iota(jnp.int32, (PAIR_WIDTH, PAIR_WIDTH), 0) // HEAD_DIM
    blk_c = lax.broadcasted_iota(jnp.int32, (PAIR_WIDTH, PAIR_WIDTH), 1) // HEAD_DIM
    same_head = blk_r == blk_c
    ones_row = jnp.ones((1, HEAD_DIM), F32)

    for p in range(N_PAIRS):
        base = p * PAIR_COLS
        qk_cols = slice(p * 2 * PAIR_WIDTH, (p + 1) * 2 * PAIR_WIDTH)
        conv = jnp.broadcast_to(cb_ref[:, qk_cols], (ts, 2 * PAIR_WIDTH))
        for k in range(QK_CONV):
            start = QK_HALO - (QK_CONV - 1) + k
            conv = conv + cw_ref[k:k + 1, qk_cols] * zbuf[start:start + ts,
                                                          base:base + 2 * PAIR_WIDTH]
        qkbuf[:, qk_cols] = _silu(conv)

        c_prev = c_state[p]
        n_prev = n_state[p]
        m_prev = [m_state[2 * p + e][:, 0:1] for e in range(2)]
        for c in range(n_chunks):
            rows = slice(c * CHUNK, (c + 1) * CHUNK)
            zrows = slice(QK_HALO + c * CHUNK, QK_HALO + (c + 1) * CHUNK)
            q2 = qkbuf[rows, p * 2 * PAIR_WIDTH: p * 2 * PAIR_WIDTH + PAIR_WIDTH]
            k2 = qkbuf[rows, p * 2 * PAIR_WIDTH + PAIR_WIDTH: (p + 1) * 2 * PAIR_WIDTH] * scale
            v2 = zbuf[zrows, base + 2 * PAIR_WIDTH: base + 3 * PAIR_WIDTH]
            og2 = zbuf[zrows, base + 3 * PAIR_WIDTH: base + 4 * PAIR_WIDTH]
            q_h = _halves(q2)
            k_h = _halves(k2)
            log_d, m, inter, a_end, s_old, s_new, m_new = [], [], [], [], [], [], []
            for e in range(2):
                hd = 2 * p + e
                bc = b_c[rows, hd:hd + 1]
                ic = ig_c[rows, hd:hd + 1]
                br = b_r[hd:hd + 1, rows]
                ir = ig_r[hd:hd + 1, rows]
                ld = jnp.where(causal, bc - br + ir, NEG)
                m_intra = jnp.max(ld, axis=1, keepdims=True)
                total = bc[CHUNK - 1:CHUNK, :]
                w_end = total - bc + ic
                m_loc = jnp.max(w_end, axis=0, keepdims=True)
                log_inter = bc + m_prev[e]
                m_e = jnp.maximum(m_intra, log_inter)
                mn = jnp.maximum(total + m_prev[e], m_loc)
                log_d.append(ld - m_e)
                m.append(m_e)
                inter.append(jnp.exp(log_inter - m_e))
                a_end.append(jnp.exp(w_end - m_loc))
                s_old.append(jnp.exp(total + m_prev[e] - mn))
                s_new.append(jnp.exp(m_loc - mn))
                m_new.append(mn)

            s2 = _dot_nt(q2, _block_diag(k2)) * jnp.exp(_pair(log_d[0], log_d[1]))
            qc2 = _dot(q2, c_prev)
            sv2 = _dot(s2, _block_diag(v2))
            s_h = _halves(s2)
            n_h = _halves(n_prev)
            hm = []
            for e in range(2):
                num = _halves(sv2)[e] + inter[e] * _halves(qc2)[e]
                den = (jnp.sum(s_h[e], axis=1, keepdims=True)
                       + inter[e] * jnp.sum(q_h[e] * n_h[e], axis=1, keepdims=True))
                hm.append(num / jnp.maximum(jnp.abs(den), jnp.exp(-m[e])))
            hm2 = _pair_head_norm(_sigmoid(og2) * _pair(hm[0], hm[1]), mng_ref[p:p + 1, :])
            mixbuf[rows, p * PAIR_WIDTH:(p + 1) * PAIR_WIDTH] = hm2.astype(BF16)

            ak2 = _pair(a_end[0] * k_h[0], a_end[1] * k_h[1])
            kv2 = jnp.where(same_head, _dot_tn(ak2, v2), 0.0)
            s_old2 = _pair(s_old[0] * ones_row, s_old[1] * ones_row)
            s_new2 = _pair(s_new[0] * ones_row, s_new[1] * ones_row)
            c_prev = s_old2 * c_prev + s_new2 * kv2
            n_prev = s_old2 * n_prev + s_new2 * jnp.sum(ak2, axis=0, keepdims=True)
            m_prev = m_new
        c_state[p] = c_prev
        n_state[p] = n_prev
        for e in range(2):
            m_state[2 * p + e] = m_prev[e] * ones_row

    for p in range(N_PAIRS):
        base = (N_PAIRS + p) * PAIR_COLS
        decay, w_end_r, q_decay, chunk_decay = [], [], [], []
        for e in range(2):
            lg = ret_log_decay[2 * p + e]
            decay.append(jnp.where(causal, jnp.exp(lg * jnp.maximum(rel, 0.0)), 0.0))
            w_end_r.append(jnp.exp(lg * (CHUNK - 1.0 - pos_c)) * ones_row)
            q_decay.append(jnp.exp(lg * (pos_c + 1.0)) * ones_row)
            chunk_decay.append(math.exp(lg * CHUNK) * ones_row)
        decay2 = _pair(*decay)
        w_end2 = _pair(*w_end_r)
        q_decay2 = _pair(*q_decay)
        chunk_decay2 = _pair(*chunk_decay)
        r_prev = r_state[p]
        for c in range(n_chunks):
            rows = slice(c * CHUNK, (c + 1) * CHUNK)
            zrows = slice(QK_HALO + c * CHUNK, QK_HALO + (c + 1) * CHUNK)
            cos = cos_ref[rows, :]
            sin = sin_ref[rows, :]
            q2 = _rotary_pair(zbuf[zrows, base:base + PAIR_WIDTH], cos, sin)
            k2 = _rotary_pair(zbuf[zrows, base + PAIR_WIDTH:base + 2 * PAIR_WIDTH], cos, sin) * scale
            v2 = zbuf[zrows, base + 2 * PAIR_WIDTH: base + 3 * PAIR_WIDTH]
            rg2 = zbuf[zrows, base + 3 * PAIR_WIDTH: base + 4 * PAIR_WIDTH]
            s2 = _dot_nt(q2, _block_diag(k2)) * decay2
            o2 = _dot(s2, _block_diag(v2)) + _dot(q2, r_prev) * q_decay2
            r_prev = chunk_decay2 * r_prev + jnp.where(same_head, _dot_tn(w_end2 * k2, v2), 0.0)
            hr2 = _silu(rg2) * _pair_head_norm(o2, rng_ref[p:p + 1, :])
            mixbuf[rows, (N_PAIRS + p) * PAIR_WIDTH:(N_PAIRS + p + 1) * PAIR_WIDTH] = hr2.astype(BF16)
        r_state[p] = r_prev

    o_ref[...] = x + jnp.dot(mixbuf[...], wout_ref[...], preferred_element_type=F32)
    zbuf[0:QK_HALO, :] = zbuf[ts:ts + QK_HALO, :]


def _retention_log_decays():
    h = np.arange(N_HEADS, dtype=np.float32)
    lg = np.log(np.float32(1.0) - np.float32(2.0) ** (np.float32(-5.0) - h)).astype(np.float32)
    return tuple(float(v) for v in lg)


def _rotary_tables(seq):
    inv = ROPE_BASE ** (-jnp.arange(0, HEAD_DIM, 2, dtype=F32) / HEAD_DIM)
    ang = jnp.arange(seq, dtype=F32)[:, None] * inv[None, :]
    cos = jnp.cos(ang)
    sin = jnp.sin(ang)
    return jnp.concatenate([cos, cos], axis=-1), jnp.concatenate([-sin, sin], axis=-1)


def _pair_major(w, n_kinds):
    lead = w.shape[:-1]
    w = w.reshape(lead + (n_kinds, N_PAIRS, PAIR_WIDTH))
    return jnp.swapaxes(w, -3, -2).reshape(lead + (n_kinds * N_PAIRS * PAIR_WIDTH,))


def _even_mixer(x2, batch, seq, norm_g, w_in, qk_conv_w, qk_conv_b, i_bias, f_bias,
                mlstm_norm_g, ret_norm_g, w_out):
    n_tok, d = x2.shape
    width = N_HEADS * HEAD_DIM
    ts = SEQ_TILE
    nt = seq // ts
    assert nt * ts == seq and ts % CHUNK == 0
    g0 = 4 * width
    g1 = g0 + 2 * N_HEADS
    w_big = jnp.concatenate([_pair_major(w_in[:, :g0], 4), _pair_major(w_in[:, g1:], 4)],
                            axis=1).astype(BF16)
    w_gate = w_in[:, g0:g1].astype(BF16)
    w_gate_c = jnp.pad(w_gate, ((0, 0), (0, HEAD_DIM - 2 * N_HEADS)))
    w_gate_r = w_gate.T
    gate_b = jnp.concatenate([i_bias, f_bias]).astype(F32)
    gate_b_c = jnp.pad(gate_b, (0, HEAD_DIM - 2 * N_HEADS)).reshape(1, HEAD_DIM)
    gate_b_r = gate_b.reshape(2 * N_HEADS, 1)
    conv_w = _pair_major(qk_conv_w, 2)
    conv_b = _pair_major(qk_conv_b, 2).reshape(1, 2 * width)
    cos_t, sin_t = _rotary_tables(seq)

    row = lambda b, i: (b * nt + i, 0)
    seq_row = lambda b, i: (i, 0)
    c2 = lambda b, i: (0, 0)
    return pl.pallas_call(
        functools.partial(_even_kernel, ret_log_decay=_retention_log_decays()),
        grid=(batch, nt),
        in_specs=[
            pl.BlockSpec((ts, d), row),
            pl.BlockSpec((1, d), c2),
            pl.BlockSpec(w_big.shape, c2),
            pl.BlockSpec(w_gate_c.shape, c2),
            pl.BlockSpec(w_gate_r.shape, c2),
            pl.BlockSpec(gate_b_c.shape, c2),
            pl.BlockSpec(gate_b_r.shape, c2),
            pl.BlockSpec(conv_w.shape, c2),
            pl.BlockSpec((1, 2 * width), c2),
            pl.BlockSpec((ts, HEAD_DIM), seq_row),
            pl.BlockSpec((ts, HEAD_DIM), seq_row),
            pl.BlockSpec((N_PAIRS, PAIR_WIDTH), c2),
            pl.BlockSpec((N_PAIRS, PAIR_WIDTH), c2),
            pl.BlockSpec(w_out.shape, c2),
        ],
        out_specs=pl.BlockSpec((ts, d), row),
        out_shape=jax.ShapeDtypeStruct((n_tok, d), F32),
        scratch_shapes=[
            pltpu.VMEM((ts + QK_HALO, 8 * width), F32),
            pltpu.VMEM((ts, 2 * width), F32),
            pltpu.VMEM((ts, 2 * width), BF16),
            pltpu.VMEM((N_PAIRS, PAIR_WIDTH, PAIR_WIDTH), F32),
            pltpu.VMEM((N_PAIRS, 1, PAIR_WIDTH), F32),
            pltpu.VMEM((N_HEADS, 1, HEAD_DIM), F32),
            pltpu.VMEM((N_PAIRS, PAIR_WIDTH, PAIR_WIDTH), F32),
        ],
        compiler_params=pltpu.CompilerParams(
            dimension_semantics=("arbitrary", "arbitrary"), vmem_limit_bytes=VMEM_LIMIT_BYTES),
        name="even_mixer",
    )(x2, norm_g.reshape(1, d), w_big, w_gate_c, w_gate_r, gate_b_c, gate_b_r,
      conv_w, conv_b, cos_t, sin_t,
      mlstm_norm_g.reshape(N_PAIRS, PAIR_WIDTH), ret_norm_g.reshape(N_PAIRS, PAIR_WIDTH),
      w_out.astype(BF16))


def kernel(x, norm_mix_g, norm_ffn_g, final_norm_g, ev_w_in, ev_qk_conv_w, ev_qk_conv_b, ev_i_bias, ev_f_bias, ev_mlstm_norm_g, ev_ret_norm_g, ev_w_out, od_w_in, od_pool_w, od_pool_scale, od_dw_w, od_dw_b, od_conv_norm_g, od_conv_norm_b, od_w_out, ffn_w_gate, ffn_w_up, ffn_w_down):
    batch, seq, d = x.shape
    depth = norm_mix_g.shape[0]
    x2 = x.reshape(batch * seq, d)
    w_gate, w_up, w_down = ffn_w_gate, ffn_w_up, ffn_w_down
    for layer in range(depth):
        j = layer // 2
        if layer % 2 == 0:
            x2 = _even_mixer(x2, batch, seq, norm_mix_g[layer], ev_w_in[j], ev_qk_conv_w[j],
                             ev_qk_conv_b[j], ev_i_bias[j], ev_f_bias[j], ev_mlstm_norm_g[j],
                             ev_ret_norm_g[j], ev_w_out[j])
        else:
            x2 = _odd_mixer(x2, batch, seq, norm_mix_g[layer], od_w_in[j], od_pool_w[j],
                            od_pool_scale[j], od_dw_w[j], od_dw_b[j], od_conv_norm_g[j],
                            od_conv_norm_b[j], od_w_out[j])
        x2 = _ffn(x2, layer, norm_ffn_g[layer], w_gate, w_up, w_down,
                  final_norm_g, apply_final_norm=(layer == depth - 1))
    return x2.reshape(batch, seq, d)
```

```python
import functools
import math

import numpy as np
import jax
import jax.numpy as jnp
from jax import lax
from jax.experimental import pallas as pl
from jax.experimental.pallas import tpu as pltpu

F32 = jnp.float32
BF16 = jnp.bfloat16

EPS = 1e-6
NEG = -1e30
CHUNK = 128
HEAD_DIM = 128
N_HEADS = 4
QK_CONV = 4
ROPE_BASE = 10000.0
POOL_WINDOWS = (2, 4, 8, 16)
POOL_GROUP = 128
DW_KERNEL = 31

SEQ_TILE = 512
FFN_TILE = 512
FF_CHUNK = 256
QK_HALO = 8
POOL_HALO = 16
DW_HALO = 32
DW_ROWS = 64
VMEM_LIMIT_BYTES = 56 * 1024 * 1024


def _rmsnorm(x, g):
    ms = jnp.mean(x * x, axis=-1, keepdims=True)
    return (x * lax.rsqrt(ms + EPS)) * g


def _normalize(x):
    xc = x - jnp.mean(x, axis=-1, keepdims=True)
    return xc * lax.rsqrt(jnp.mean(xc * xc, axis=-1, keepdims=True) + EPS)


def _sigmoid(x):
    return 1.0 / (1.0 + jnp.exp(-x))


def _silu(x):
    return x * _sigmoid(x)


def _log_sigmoid(x):
    return jnp.minimum(x, 0.0) - jnp.log(1.0 + jnp.exp(-jnp.abs(x)))


def _dot(a, b):
    return jnp.dot(a.astype(BF16), b.astype(BF16), preferred_element_type=F32)


def _dot_nt(a, b):
    return lax.dot_general(a.astype(BF16), b.astype(BF16), (((1,), (1,)), ((), ())),
                           preferred_element_type=F32)


def _dot_tn(a, b):
    return lax.dot_general(a.astype(BF16), b.astype(BF16), (((0,), (0,)), ((), ())),
                           preferred_element_type=F32)


def _ffn_kernel(x_ref, g_ref, wg_ref, wu_ref, wd_ref, fg_ref, o_ref, *, apply_final_norm):
    x = x_ref[...]
    h = _rmsnorm(x, g_ref[...]).astype(BF16)
    acc = x
    d_ff = wg_ref.shape[1]
    for c0 in range(0, d_ff, FF_CHUNK):
        cols = slice(c0, c0 + FF_CHUNK)
        gate = jnp.dot(h, wg_ref[:, cols].astype(BF16), preferred_element_type=F32)
        up = jnp.dot(h, wu_ref[:, cols].astype(BF16), preferred_element_type=F32)
        a = (_silu(gate) * up).astype(BF16)
        acc = acc + jnp.dot(a, wd_ref[cols, :].astype(BF16), preferred_element_type=F32)
    if apply_final_norm:
        acc = _rmsnorm(acc, fg_ref[...])
    o_ref[...] = acc


def _ffn(x2, layer, norm_g, w_gate, w_up, w_down, final_g, apply_final_norm):
    n_tok, d = x2.shape
    d_ff = w_gate.shape[2]
    assert d_ff % FF_CHUNK == 0 and n_tok % FFN_TILE == 0
    const2 = lambda i: (0, 0)
    pick = lambda i: (layer, 0, 0)
    return pl.pallas_call(
        functools.partial(_ffn_kernel, apply_final_norm=apply_final_norm),
        grid=(n_tok // FFN_TILE,),
        in_specs=[
            pl.BlockSpec((FFN_TILE, d), lambda i: (i, 0)),
            pl.BlockSpec((1, d), const2),
            pl.BlockSpec((None, d, d_ff), pick, pipeline_mode=pl.Buffered(1)),
            pl.BlockSpec((None, d, d_ff), pick, pipeline_mode=pl.Buffered(1)),
            pl.BlockSpec((None, d_ff, d), pick, pipeline_mode=pl.Buffered(1)),
            pl.BlockSpec((1, d), const2),
        ],
        out_specs=pl.BlockSpec((FFN_TILE, d), lambda i: (i, 0)),
        out_shape=jax.ShapeDtypeStruct((n_tok, d), F32),
        compiler_params=pltpu.CompilerParams(
            dimension_semantics=("arbitrary",), vmem_limit_bytes=VMEM_LIMIT_BYTES),
        name="ffn",
    )(x2, norm_g.reshape(1, d), w_gate, w_up, w_down, final_g.reshape(1, d))


def _odd_kernel(x_ref, g_ref, win_ref, poolw_ref, pscale_ref, dww_ref, dwb_ref, cng_ref, cnb_ref,
                wout_ref, o_ref, ubuf, ybuf, yrot, cbuf, *, pool_width):
    i = pl.program_id(1)
    ts = x_ref.shape[0]

    @pl.when(i == 0)
    def _():
        ubuf[0:POOL_HALO, :] = jnp.zeros((POOL_HALO, ubuf.shape[1]), F32)
        ybuf[0:DW_HALO, :] = jnp.zeros((DW_HALO, ybuf.shape[1]), F32)

    x = x_ref[...]
    h = _rmsnorm(x, g_ref[...]).astype(BF16)
    z = jnp.dot(h, win_ref[...], preferred_element_type=F32)
    conv_width = (z.shape[1] - pool_width) // 2
    ubuf[POOL_HALO:POOL_HALO + ts, :] = z[:, :pool_width]
    ga = z[:, pool_width:pool_width + conv_width]
    gb = z[:, pool_width + conv_width:]
    ybuf[DW_HALO:DW_HALO + ts, :] = ga * _sigmoid(gb)

    pos = (i * ts + lax.broadcasted_iota(jnp.int32, (ts, 1), 0)).astype(F32) + 1.0
    out = x
    for g, w in enumerate(POOL_WINDOWS):
        cols = slice(g * POOL_GROUP, (g + 1) * POOL_GROUP)
        wsum = ubuf[:, cols]
        shift = 1
        while shift < w:
            wsum = wsum + pltpu.roll(wsum, shift, axis=0)
            shift *= 2
        cur = ubuf[POOL_HALO:POOL_HALO + ts, cols]
        pooled = wsum[POOL_HALO:, :] / jnp.minimum(pos, float(w)) - cur
        y_pool = _dot(pooled, poolw_ref[g]) * pscale_ref[:, cols]
        out = out + _dot(y_pool, wout_ref[cols, :])

    rot_rows = yrot.shape[1]
    for r in range(1, 8):
        yrot[r - 1] = ybuf[r:r + rot_rows, :]
    for r0 in range(0, ts, DW_ROWS):
        for c0 in range(0, conv_width, 128):
            cols = slice(c0, c0 + 128)
            acc = jnp.broadcast_to(dwb_ref[:, cols], (DW_ROWS, 128))
            for k in range(DW_KERNEL):
                start = DW_HALO - (DW_KERNEL - 1) + k + r0
                src = ybuf if start % 8 == 0 else yrot.at[start % 8 - 1]
                aligned = start - start % 8
                acc = acc + dww_ref[k:k + 1, cols] * src[aligned:aligned + DW_ROWS, cols]
            cbuf[r0:r0 + DW_ROWS, cols] = acc
    yf = _normalize(cbuf[...]) * cng_ref[...] + cnb_ref[...]
    y_conv = _silu(yf)
    out = out + _dot(y_conv, wout_ref[pool_width:, :])
    o_ref[...] = out

    ubuf[0:POOL_HALO, :] = ubuf[ts:ts + POOL_HALO, :]
    ybuf[0:DW_HALO, :] = ybuf[ts:ts + DW_HALO, :]


def _odd_mixer(x2, batch, seq, norm_g, w_in, pool_w, pool_scale, dw_w, dw_b, cn_g, cn_b, w_out):
    n_tok, d = x2.shape
    pool_width = pool_w.shape[0] * pool_w.shape[1]
    conv_width = dw_w.shape[1]
    ts = SEQ_TILE
    nt = seq // ts
    assert nt * ts == seq
    row = lambda b, i: (b * nt + i, 0)
    c2 = lambda b, i: (0, 0)
    c3 = lambda b, i: (0, 0, 0)
    return pl.pallas_call(
        functools.partial(_odd_kernel, pool_width=pool_width),
        grid=(batch, nt),
        in_specs=[
            pl.BlockSpec((ts, d), row),
            pl.BlockSpec((1, d), c2),
            pl.BlockSpec(w_in.shape, c2),
            pl.BlockSpec(pool_w.shape, c3),
            pl.BlockSpec((1, pool_width), c2),
            pl.BlockSpec(dw_w.shape, c2),
            pl.BlockSpec((1, conv_width), c2),
            pl.BlockSpec((1, conv_width), c2),
            pl.BlockSpec((1, conv_width), c2),
            pl.BlockSpec(w_out.shape, c2),
        ],
        out_specs=pl.BlockSpec((ts, d), row),
        out_shape=jax.ShapeDtypeStruct((n_tok, d), F32),
        scratch_shapes=[
            pltpu.VMEM((ts + POOL_HALO, pool_width), F32),
            pltpu.VMEM((ts + DW_HALO, conv_width), F32),
            pltpu.VMEM((7, ts + DW_HALO - 8, conv_width), F32),
            pltpu.VMEM((ts, conv_width), F32),
        ],
        compiler_params=pltpu.CompilerParams(
            dimension_semantics=("arbitrary", "arbitrary"), vmem_limit_bytes=VMEM_LIMIT_BYTES),
        name="odd_mixer",
    )(x2, norm_g.reshape(1, d), w_in.astype(BF16), pool_w.astype(BF16),
      pool_scale.reshape(1, pool_width), dw_w, dw_b.reshape(1, conv_width),
      cn_g.reshape(1, conv_width), cn_b.reshape(1, conv_width), w_out.astype(BF16))


PAIR_WIDTH = 2 * HEAD_DIM
N_PAIRS = N_HEADS // 2
PAIR_COLS = 4 * PAIR_WIDTH


def _chunk_cumsum_rows(x, n):
    idx = lax.broadcasted_iota(jnp.int32, x.shape, 0) % n
    s = 1
    while s < n:
        x = x + jnp.where(idx >= s, pltpu.roll(x, s, axis=0), 0.0)
        s *= 2
    return x


def _chunk_cumsum_lanes(x, n):
    idx = lax.broadcasted_iota(jnp.int32, x.shape, 1) % n
    s = 1
    while s < n:
        x = x + jnp.where(idx >= s, pltpu.roll(x, s, axis=1), 0.0)
        s *= 2
    return x


def _halves(a2):
    return a2[:, :HEAD_DIM], a2[:, HEAD_DIM:]


def _pair(a0, a1):
    return jnp.concatenate([a0, a1], axis=1)


def _block_diag(a2):
    a0, a1 = _halves(a2)
    z = jnp.zeros_like(a0)
    return jnp.concatenate([_pair(a0, z), _pair(z, a1)], axis=0)


def _pair_head_norm(x2, g2):
    x0, x1 = _halves(x2)
    return _pair(_normalize(x0), _normalize(x1)) * g2


def _rotary_pair(t2, cos, sin):
    t0, t1 = _halves(t2)
    return _pair(t0 * cos + pltpu.roll(t0, HEAD_DIM // 2, axis=1) * sin,
                 t1 * cos + pltpu.roll(t1, HEAD_DIM // 2, axis=1) * sin)


def _even_kernel(x_ref, g_ref, wbig_ref, wgc_ref, wgr_ref, gbc_ref, gbr_ref, cw_ref, cb_ref,
                 cos_ref, sin_ref, mng_ref, rng_ref, wout_ref, o_ref,
                 zbuf, qkbuf, mixbuf, c_state, n_state, m_state, r_state, *, ret_log_decay):
    i = pl.program_id(1)
    ts = x_ref.shape[0]
    scale = HEAD_DIM ** -0.5
    n_chunks = ts // CHUNK

    @pl.when(i == 0)
    def _():
        zbuf[0:QK_HALO, :] = jnp.zeros((QK_HALO, zbuf.shape[1]), F32)
        c_state[...] = jnp.zeros(c_state.shape, F32)
        n_state[...] = jnp.zeros(n_state.shape, F32)
        m_state[...] = jnp.full(m_state.shape, NEG, F32)
        r_state[...] = jnp.zeros(r_state.shape, F32)

    x = x_ref[...]
    h = _rmsnorm(x, g_ref[...]).astype(BF16)
    width = N_HEADS * HEAD_DIM
    for pp in range(2 * N_PAIRS):
        for j in range(PAIR_COLS // PAIR_WIDTH):
            src = (pp // N_PAIRS) * 4 * width + j * width + (pp % N_PAIRS) * PAIR_WIDTH
            dst = pp * PAIR_COLS + j * PAIR_WIDTH
            zbuf[QK_HALO:QK_HALO + ts, dst:dst + PAIR_WIDTH] = jnp.dot(
                h, wbig_ref[:, src:src + PAIR_WIDTH], preferred_element_type=F32)

    gate_c = jnp.dot(h, wgc_ref[...], preferred_element_type=F32) + gbc_ref[...]
    gate_r = _dot_nt(wgr_ref[...], h) + gbr_ref[...]
    ig_c = gate_c[:, 0:N_HEADS]
    b_c = _chunk_cumsum_rows(_log_sigmoid(gate_c), CHUNK)[:, N_HEADS:2 * N_HEADS]
    ig_r = gate_r[0:N_HEADS, :]
    b_r = _chunk_cumsum_lanes(_log_sigmoid(gate_r), CHUNK)[N_HEADS:2 * N_HEADS, :]

    row_i = lax.broadcasted_iota(jnp.int32, (CHUNK, CHUNK), 0)
    col_j = lax.broadcasted_iota(jnp.int32, (CHUNK, CHUNK), 1)
    causal = col_j <= row_i
    rel = (row_i - col_j).astype(F32)
    pos_c = lax.broadcasted_iota(jnp.int32, (CHUNK, 1), 0).astype(F32)
    blk_r = lax.broadcasted_iota(jnp.int32, (PAIR_WIDTH, PAIR_WIDTH), 0) // HEAD_DIM
    blk_c = lax.broadcasted_iota(jnp.int32, (PAIR_WIDTH, PAIR_WIDTH), 1) // HEAD_DIM
    same_head = blk_r == blk_c
    ones_row = jnp.ones((1, HEAD_DIM), F32)

    for p in range(N_PAIRS):
        base = p * PAIR_COLS
        qk_cols = slice(p * 2 * PAIR_WIDTH, (p + 1) * 2 * PAIR_WIDTH)
        q_cols = slice(p * PAIR_WIDTH, (p + 1) * PAIR_WIDTH)
        k_cols = slice(width + p * PAIR_WIDTH, width + (p + 1) * PAIR_WIDTH)
        conv = jnp.broadcast_to(_pair(cb_ref[:, q_cols], cb_ref[:, k_cols]), (ts, 2 * PAIR_WIDTH))
        for k in range(QK_CONV):
            start = QK_HALO - (QK_CONV - 1) + k
            w_k = _pair(cw_ref[k:k + 1, q_cols], cw_ref[k:k + 1, k_cols])
            conv = conv + w_k * zbuf[start:start + ts, base:base + 2 * PAIR_WIDTH]
        qkbuf[:, qk_cols] = _silu(conv)

    def mlstm_a(p, c, st):
        base = p * PAIR_COLS
        rows = slice(c * CHUNK, (c + 1) * CHUNK)
        zrows = slice(QK_HALO + c * CHUNK, QK_HALO + (c + 1) * CHUNK)
        q2 = qkbuf[rows, p * 2 * PAIR_WIDTH: p * 2 * PAIR_WIDTH + PAIR_WIDTH]
        k2 = qkbuf[rows, p * 2 * PAIR_WIDTH + PAIR_WIDTH: (p + 1) * 2 * PAIR_WIDTH] * scale
        v2 = zbuf[zrows, base + 2 * PAIR_WIDTH: base + 3 * PAIR_WIDTH]
        k_h = _halves(k2)
        log_d, m, inter, a_end, s_old, s_new, m_new = [], [], [], [], [], [], []
        for e in range(2):
            hd = 2 * p + e
            m_prev = st["m"][e]
            bc = b_c[rows, hd:hd + 1]
            ic = ig_c[rows, hd:hd + 1]
            br = b_r[hd:hd + 1, rows]
            ir = ig_r[hd:hd + 1, rows]
            ld = jnp.where(causal, bc - br + ir, NEG)
            m_intra = jnp.max(ld, axis=1, keepdims=True)
            total = bc[CHUNK - 1:CHUNK, :]
            w_end = total - bc + ic
            m_loc = jnp.max(w_end, axis=0, keepdims=True)
            log_inter = bc + m_prev
            m_e = jnp.maximum(m_intra, log_inter)
            mn = jnp.maximum(total + m_prev, m_loc)
            log_d.append(ld - m_e)
            m.append(m_e)
            inter.append(jnp.exp(log_inter - m_e))
            a_end.append(jnp.exp(w_end - m_loc))
            s_old.append(jnp.exp(total + m_prev - mn))
            s_new.append(jnp.exp(m_loc - mn))
            m_new.append(mn)
        ak2 = _pair(a_end[0] * k_h[0], a_end[1] * k_h[1])
        return dict(rows=rows, zrows=zrows, q2=q2, v2=v2, m=m, inter=inter,
                    decay=jnp.exp(_pair(log_d[0], log_d[1])),
                    qk=_dot_nt(q2, _block_diag(k2)),
                    q_state=_dot(q2, st["c"]),
                    kv=_dot_tn(ak2, v2),
                    n_sum=jnp.sum(ak2, axis=0, keepdims=True),
                    s_old2=_pair(s_old[0] * ones_row, s_old[1] * ones_row),
                    s_new2=_pair(s_new[0] * ones_row, s_new[1] * ones_row),
                    m_new=m_new)

    def mlstm_c(p, st, t):
        base = p * PAIR_COLS
        og2 = zbuf[t["zrows"], base + 3 * PAIR_WIDTH: base + 4 * PAIR_WIDTH]
        s_h = _halves(t["s2"])
        q_h = _halves(t["q2"])
        n_h = _halves(st["n"])
        hm = []
        for e in range(2):
            num = _halves(t["sv"])[e] + t["inter"][e] * _halves(t["q_state"])[e]
            den = (jnp.sum(s_h[e], axis=1, keepdims=True)
                   + t["inter"][e] * jnp.sum(q_h[e] * n_h[e], axis=1, keepdims=True))
            hm.append(num / jnp.maximum(jnp.abs(den), jnp.exp(-t["m"][e])))
        g2 = _pair(mng_ref[2 * p:2 * p + 1, :], mng_ref[2 * p + 1:2 * p + 2, :])
        hm2 = _pair_head_norm(_sigmoid(og2) * _pair(hm[0], hm[1]), g2)
        mixbuf[t["rows"], p * PAIR_WIDTH:(p + 1) * PAIR_WIDTH] = hm2.astype(BF16)
        st["c"] = t["s_old2"] * st["c"] + t["s_new2"] * jnp.where(same_head, t["kv"], 0.0)
        st["n"] = t["s_old2"] * st["n"] + t["s_new2"] * t["n_sum"]
        st["m"] = t["m_new"]

    def ret_a(p, c, st):
        base = (N_PAIRS + p) * PAIR_COLS
        rows = slice(c * CHUNK, (c + 1) * CHUNK)
        zrows = slice(QK_HALO + c * CHUNK, QK_HALO + (c + 1) * CHUNK)
        cos = cos_ref[rows, :]
        sin = sin_ref[rows, :]
        q2 = _rotary_pair(zbuf[zrows, base:base + PAIR_WIDTH], cos, sin)
        k2 = _rotary_pair(zbuf[zrows, base + PAIR_WIDTH:base + 2 * PAIR_WIDTH], cos, sin) * scale
        v2 = zbuf[zrows, base + 2 * PAIR_WIDTH: base + 3 * PAIR_WIDTH]
        return dict(rows=rows, zrows=zrows, v2=v2, decay=st["decay2"],
                    qk=_dot_nt(q2, _block_diag(k2)),
                    q_state=_dot(q2, st["r"]),
                    kv=_dot_tn(st["w_end2"] * k2, v2))

    def ret_c(p, st, t):
        base = (N_PAIRS + p) * PAIR_COLS
        rg2 = zbuf[t["zrows"], base + 3 * PAIR_WIDTH: base + 4 * PAIR_WIDTH]
        o2 = t["sv"] + t["q_state"] * st["q_decay2"]
        g2 = _pair(rng_ref[2 * p:2 * p + 1, :], rng_ref[2 * p + 1:2 * p + 2, :])
        hr2 = _silu(rg2) * _pair_head_norm(o2, g2)
        mixbuf[t["rows"], (N_PAIRS + p) * PAIR_WIDTH:(N_PAIRS + p + 1) * PAIR_WIDTH] = hr2.astype(BF16)
        st["r"] = st["chunk_decay2"] * st["r"] + jnp.where(same_head, t["kv"], 0.0)

    def stage_b(t):
        t["s2"] = t["qk"] * t["decay"]
        t["sv"] = _dot(t["s2"], _block_diag(t["v2"]))

    lanes = []
    for p in range(N_PAIRS):
        lanes.append((functools.partial(mlstm_a, p), functools.partial(mlstm_c, p),
                      dict(c=c_state[p], n=n_state[p],
                           m=[m_state[2 * p + e][:, 0:1] for e in range(2)])))
    for p in range(N_PAIRS):
        decay, w_end_r, q_decay, chunk_decay = [], [], [], []
        for e in range(2):
            lg = ret_log_decay[2 * p + e]
            decay.append(jnp.where(causal, jnp.exp(lg * jnp.maximum(rel, 0.0)), 0.0))
            w_end_r.append(jnp.exp(lg * (CHUNK - 1.0 - pos_c)) * ones_row)
            q_decay.append(jnp.exp(lg * (pos_c + 1.0)) * ones_row)
            chunk_decay.append(math.exp(lg * CHUNK) * ones_row)
        lanes.append((functools.partial(ret_a, p), functools.partial(ret_c, p),
                      dict(r=r_state[p], decay2=_pair(*decay), w_end2=_pair(*w_end_r),
                           q_decay2=_pair(*q_decay), chunk_decay2=_pair(*chunk_decay))))

    for c in range(n_chunks):
        work = [stage_a(c, st) for stage_a, _, st in lanes]
        for t in work:
            stage_b(t)
        for (_, stage_c, st), t in zip(lanes, work):
            stage_c(st, t)

    for p in range(N_PAIRS):
        st = lanes[p][2]
        c_state[p] = st["c"]
        n_state[p] = st["n"]
        for e in range(2):
            m_state[2 * p + e] = st["m"][e] * ones_row
        r_state[p] = lanes[N_PAIRS + p][2]["r"]

    o_ref[...] = x + jnp.dot(mixbuf[...], wout_ref[...], preferred_element_type=F32)
    zbuf[0:QK_HALO, :] = zbuf[ts:ts + QK_HALO, :]


def _retention_log_decays():
    h = np.arange(N_HEADS, dtype=np.float32)
    lg = np.log(np.float32(1.0) - np.float32(2.0) ** (np.float32(-5.0) - h)).astype(np.float32)
    return tuple(float(v) for v in lg)


@functools.lru_cache(maxsize=None)
def _rotary_tables(seq):
    inv = np.float32(ROPE_BASE) ** (-np.arange(0, HEAD_DIM, 2, dtype=np.float32) / np.float32(HEAD_DIM))
    ang = np.arange(seq, dtype=np.float32)[:, None] * inv[None, :]
    cos = np.cos(ang).astype(np.float32)
    sin = np.sin(ang).astype(np.float32)
    return np.concatenate([cos, cos], axis=-1), np.concatenate([-sin, sin], axis=-1)


def _even_mixer(x2, batch, seq, norm_g, w_in, qk_conv_w, qk_conv_b, i_bias, f_bias,
                mlstm_norm_g, ret_norm_g, w_out):
    n_tok, d = x2.shape
    width = N_HEADS * HEAD_DIM
    ts = SEQ_TILE
    nt = seq // ts
    assert nt * ts == seq and ts % CHUNK == 0
    g0 = 4 * width
    g1 = g0 + 2 * N_HEADS
    w_big = jnp.concatenate([w_in[:, :g0], w_in[:, g1:]], axis=1).astype(BF16)
    w_gate = w_in[:, g0:g1].astype(BF16)
    w_gate_c = jnp.pad(w_gate, ((0, 0), (0, HEAD_DIM - 2 * N_HEADS)))
    w_gate_r = w_gate.T
    gate_b = jnp.concatenate([i_bias, f_bias]).astype(F32)
    gate_b_c = jnp.pad(gate_b, (0, HEAD_DIM - 2 * N_HEADS)).reshape(1, HEAD_DIM)
    gate_b_r = gate_b.reshape(2 * N_HEADS, 1)
    cos_t, sin_t = _rotary_tables(seq)

    row = lambda b, i: (b * nt + i, 0)
    seq_row = lambda b, i: (i, 0)
    c2 = lambda b, i: (0, 0)
    return pl.pallas_call(
        functools.partial(_even_kernel, ret_log_decay=_retention_log_decays()),
        grid=(batch, nt),
        in_specs=[
            pl.BlockSpec((ts, d), row),
            pl.BlockSpec((1, d), c2),
            pl.BlockSpec(w_big.shape, c2),
            pl.BlockSpec(w_gate_c.shape, c2),
            pl.BlockSpec(w_gate_r.shape, c2),
            pl.BlockSpec(gate_b_c.shape, c2),
            pl.BlockSpec(gate_b_r.shape, c2),
            pl.BlockSpec(qk_conv_w.shape, c2),
            pl.BlockSpec((1, 2 * width), c2),
            pl.BlockSpec((ts, HEAD_DIM), seq_row),
            pl.BlockSpec((ts, HEAD_DIM), seq_row),
            pl.BlockSpec(mlstm_norm_g.shape, c2),
            pl.BlockSpec(ret_norm_g.shape, c2),
            pl.BlockSpec(w_out.shape, c2),
        ],
        out_specs=pl.BlockSpec((ts, d), row),
        out_shape=jax.ShapeDtypeStruct((n_tok, d), F32),
        scratch_shapes=[
            pltpu.VMEM((ts + QK_HALO, 8 * width), F32),
            pltpu.VMEM((ts, 2 * width), F32),
            pltpu.VMEM((ts, 2 * width), BF16),
            pltpu.VMEM((N_PAIRS, PAIR_WIDTH, PAIR_WIDTH), F32),
            pltpu.VMEM((N_PAIRS, 1, PAIR_WIDTH), F32),
            pltpu.VMEM((N_HEADS, 1, HEAD_DIM), F32),
            pltpu.VMEM((N_PAIRS, PAIR_WIDTH, PAIR_WIDTH), F32),
        ],
        compiler_params=pltpu.CompilerParams(
            dimension_semantics=("arbitrary", "arbitrary"), vmem_limit_bytes=VMEM_LIMIT_BYTES),
        name="even_mixer",
    )(x2, norm_g.reshape(1, d), w_big, w_gate_c, w_gate_r, gate_b_c, gate_b_r,
      qk_conv_w, qk_conv_b.reshape(1, 2 * width), jnp.asarray(cos_t), jnp.asarray(sin_t),
      mlstm_norm_g, ret_norm_g, w_out.astype(BF16))


def kernel(x, norm_mix_g, norm_ffn_g, final_norm_g, ev_w_in, ev_qk_conv_w, ev_qk_conv_b, ev_i_bias, ev_f_bias, ev_mlstm_norm_g, ev_ret_norm_g, ev_w_out, od_w_in, od_pool_w, od_pool_scale, od_dw_w, od_dw_b, od_conv_norm_g, od_conv_norm_b, od_w_out, ffn_w_gate, ffn_w_up, ffn_w_down):
    batch, seq, d = x.shape
    depth = norm_mix_g.shape[0]
    x2 = x.reshape(batch * seq, d)
    w_gate, w_up, w_down = ffn_w_gate, ffn_w_up, ffn_w_down
    for layer in range(depth):
        j = layer // 2
        if layer % 2 == 0:
            x2 = _even_mixer(x2, batch, seq, norm_mix_g[layer], ev_w_in[j], ev_qk_conv_w[j],
                             ev_qk_conv_b[j], ev_i_bias[j], ev_f_bias[j], ev_mlstm_norm_g[j],
                             ev_ret_norm_g[j], ev_w_out[j])
        else:
            x2 = _odd_mixer(x2, batch, seq, norm_mix_g[layer], od_w_in[j], od_pool_w[j],
                            od_pool_scale[j], od_dw_w[j], od_dw_b[j], od_conv_norm_g[j],
                            od_conv_norm_b[j], od_w_out[j])
        x2 = _ffn(x2, layer, norm_ffn_g[layer], w_gate, w_up, w_down,
                  final_norm_g, apply_final_norm=(layer == depth - 1))
    return x2.reshape(batch, seq, d)
```

```python
import functools
import math

import numpy as np
import jax
import jax.numpy as jnp
from jax import lax
from jax.experimental import pallas as pl
from jax.experimental.pallas import tpu as pltpu

F32 = jnp.float32
BF16 = jnp.bfloat16

EPS = 1e-6
NEG = -1e30
CHUNK = 128
HEAD_DIM = 128
N_HEADS = 4
QK_CONV = 4
ROPE_BASE = 10000.0
POOL_WINDOWS = (2, 4, 8, 16)
POOL_GROUP = 128
DW_KERNEL = 31

SEQ_TILE = 512
FFN_TILE = 512
FF_CHUNK = 256
QK_HALO = 8
POOL_HALO = 16
DW_HALO = 32
DW_ROWS = 64
VMEM_LIMIT_BYTES = 56 * 1024 * 1024


def _rmsnorm(x, g):
    ms = jnp.mean(x * x, axis=-1, keepdims=True)
    return (x * lax.rsqrt(ms + EPS)) * g


def _normalize(x):
    xc = x - jnp.mean(x, axis=-1, keepdims=True)
    return xc * lax.rsqrt(jnp.mean(xc * xc, axis=-1, keepdims=True) + EPS)


def _sigmoid(x):
    return 1.0 / (1.0 + jnp.exp(-x))


def _silu(x):
    return x * _sigmoid(x)


def _log_sigmoid(x):
    return jnp.minimum(x, 0.0) - jnp.log(1.0 + jnp.exp(-jnp.abs(x)))


def _dot(a, b):
    return jnp.dot(a.astype(BF16), b.astype(BF16), preferred_element_type=F32)


def _dot_nt(a, b):
    return lax.dot_general(a.astype(BF16), b.astype(BF16), (((1,), (1,)), ((), ())),
                           preferred_element_type=F32)


def _dot_tn(a, b):
    return lax.dot_general(a.astype(BF16), b.astype(BF16), (((0,), (0,)), ((), ())),
                           preferred_element_type=F32)


def _ffn_kernel(x_ref, g_ref, wg_ref, wu_ref, wd_ref, fg_ref, o_ref, *, apply_final_norm):
    x = x_ref[...]
    h = _rmsnorm(x, g_ref[...]).astype(BF16)
    acc = x
    d_ff = wg_ref.shape[1]
    for c0 in range(0, d_ff, FF_CHUNK):
        cols = slice(c0, c0 + FF_CHUNK)
        gate = jnp.dot(h, wg_ref[:, cols].astype(BF16), preferred_element_type=F32)
        up = jnp.dot(h, wu_ref[:, cols].astype(BF16), preferred_element_type=F32)
        a = (_silu(gate) * up).astype(BF16)
        acc = acc + jnp.dot(a, wd_ref[cols, :].astype(BF16), preferred_element_type=F32)
    if apply_final_norm:
        acc = _rmsnorm(acc, fg_ref[...])
    o_ref[...] = acc


def _ffn(x2, layer, norm_g, w_gate, w_up, w_down, final_g, apply_final_norm):
    n_tok, d = x2.shape
    d_ff = w_gate.shape[2]
    assert d_ff % FF_CHUNK == 0 and n_tok % FFN_TILE == 0
    const2 = lambda i: (0, 0)
    pick = lambda i: (layer, 0, 0)
    return pl.pallas_call(
        functools.partial(_ffn_kernel, apply_final_norm=apply_final_norm),
        grid=(n_tok // FFN_TILE,),
        in_specs=[
            pl.BlockSpec((FFN_TILE, d), lambda i: (i, 0)),
            pl.BlockSpec((1, d), const2),
            pl.BlockSpec((None, d, d_ff), pick, pipeline_mode=pl.Buffered(1)),
            pl.BlockSpec((None, d, d_ff), pick, pipeline_mode=pl.Buffered(1)),
            pl.BlockSpec((None, d_ff, d), pick, pipeline_mode=pl.Buffered(1)),
            pl.BlockSpec((1, d), const2),
        ],
        out_specs=pl.BlockSpec((FFN_TILE, d), lambda i: (i, 0)),
        out_shape=jax.ShapeDtypeStruct((n_tok, d), F32),
        compiler_params=pltpu.CompilerParams(
            dimension_semantics=("arbitrary",), vmem_limit_bytes=VMEM_LIMIT_BYTES),
        name="ffn",
    )(x2, norm_g.reshape(1, d), w_gate, w_up, w_down, final_g.reshape(1, d))


def _odd_kernel(x_ref, g_ref, win_ref, poolw_ref, pscale_ref, dww_ref, dwb_ref, cng_ref, cnb_ref,
                wout_ref, o_ref, ubuf, ybuf, yrot, cbuf, *, pool_width):
    i = pl.program_id(1)
    ts = x_ref.shape[0]

    @pl.when(i == 0)
    def _():
        ubuf[0:POOL_HALO, :] = jnp.zeros((POOL_HALO, ubuf.shape[1]), F32)
        ybuf[0:DW_HALO, :] = jnp.zeros((DW_HALO, ybuf.shape[1]), F32)

    x = x_ref[...]
    h = _rmsnorm(x, g_ref[...]).astype(BF16)
    z = jnp.dot(h, win_ref[...], preferred_element_type=F32)
    conv_width = (z.shape[1] - pool_width) // 2
    ubuf[POOL_HALO:POOL_HALO + ts, :] = z[:, :pool_width]
    ga = z[:, pool_width:pool_width + conv_width]
    gb = z[:, pool_width + conv_width:]
    ybuf[DW_HALO:DW_HALO + ts, :] = ga * _sigmoid(gb)

    pos = (i * ts + lax.broadcasted_iota(jnp.int32, (ts, 1), 0)).astype(F32) + 1.0
    out = x
    for g, w in enumerate(POOL_WINDOWS):
        cols = slice(g * POOL_GROUP, (g + 1) * POOL_GROUP)
        wsum = ubuf[:, cols]
        shift = 1
        while shift < w:
            wsum = wsum + pltpu.roll(wsum, shift, axis=0)
            shift *= 2
        cur = ubuf[POOL_HALO:POOL_HALO + ts, cols]
        pooled = wsum[POOL_HALO:, :] / jnp.minimum(pos, float(w)) - cur
        y_pool = _dot(pooled, poolw_ref[g]) * pscale_ref[:, cols]
        out = out + _dot(y_pool, wout_ref[cols, :])

    rot_rows = yrot.shape[1]
    for r in range(1, 8):
        yrot[r - 1] = ybuf[r:r + rot_rows, :]
    for r0 in range(0, ts, DW_ROWS):
        for c0 in range(0, conv_width, 128):
            cols = slice(c0, c0 + 128)
            acc = jnp.broadcast_to(dwb_ref[:, cols], (DW_ROWS, 128))
            for k in range(DW_KERNEL):
                start = DW_HALO - (DW_KERNEL - 1) + k + r0
                src = ybuf if start % 8 == 0 else yrot.at[start % 8 - 1]
                aligned = start - start % 8
                acc = acc + dww_ref[k:k + 1, cols] * src[aligned:aligned + DW_ROWS, cols]
            cbuf[r0:r0 + DW_ROWS, cols] = acc
    yf = _normalize(cbuf[...]) * cng_ref[...] + cnb_ref[...]
    y_conv = _silu(yf)
    out = out + _dot(y_conv, wout_ref[pool_width:, :])
    o_ref[...] = out

    ubuf[0:POOL_HALO, :] = ubuf[ts:ts + POOL_HALO, :]
    ybuf[0:DW_HALO, :] = ybuf[ts:ts + DW_HALO, :]


def _odd_mixer(x2, batch, seq, norm_g, w_in, pool_w, pool_scale, dw_w, dw_b, cn_g, cn_b, w_out):
    n_tok, d = x2.shape
    pool_width = pool_w.shape[0] * pool_w.shape[1]
    conv_width = dw_w.shape[1]
    ts = SEQ_TILE
    nt = seq // ts
    assert nt * ts == seq
    row = lambda b, i: (b * nt + i, 0)
    c2 = lambda b, i: (0, 0)
    c3 = lambda b, i: (0, 0, 0)
    return pl.pallas_call(
        functools.partial(_odd_kernel, pool_width=pool_width),
        grid=(batch, nt),
        in_specs=[
            pl.BlockSpec((ts, d), row),
            pl.BlockSpec((1, d), c2),
            pl.BlockSpec(w_in.shape, c2),
            pl.BlockSpec(pool_w.shape, c3),
            pl.BlockSpec((1, pool_width), c2),
            pl.BlockSpec(dw_w.shape, c2),
            pl.BlockSpec((1, conv_width), c2),
            pl.BlockSpec((1, conv_width), c2),
            pl.BlockSpec((1, conv_width), c2),
            pl.BlockSpec(w_out.shape, c2),
        ],
        out_specs=pl.BlockSpec((ts, d), row),
        out_shape=jax.ShapeDtypeStruct((n_tok, d), F32),
        scratch_shapes=[
            pltpu.VMEM((ts + POOL_HALO, pool_width), F32),
            pltpu.VMEM((ts + DW_HALO, conv_width), F32),
            pltpu.VMEM((7, ts + DW_HALO - 8, conv_width), F32),
            pltpu.VMEM((ts, conv_width), F32),
        ],
        compiler_params=pltpu.CompilerParams(
            dimension_semantics=("arbitrary", "arbitrary"), vmem_limit_bytes=VMEM_LIMIT_BYTES),
        name="odd_mixer",
    )(x2, norm_g.reshape(1, d), w_in.astype(BF16), pool_w.astype(BF16),
      pool_scale.reshape(1, pool_width), dw_w, dw_b.reshape(1, conv_width),
      cn_g.reshape(1, conv_width), cn_b.reshape(1, conv_width), w_out.astype(BF16))


PAIR_WIDTH = 2 * HEAD_DIM
N_PAIRS = N_HEADS // 2
PAIR_COLS = 4 * PAIR_WIDTH


def _chunk_cumsum_rows(x, n):
    idx = lax.broadcasted_iota(jnp.int32, x.shape, 0) % n
    s = 1
    while s < n:
        x = x + jnp.where(idx >= s, pltpu.roll(x, s, axis=0), 0.0)
        s *= 2
    return x


def _chunk_cumsum_lanes(x, n):
    idx = lax.broadcasted_iota(jnp.int32, x.shape, 1) % n
    s = 1
    while s < n:
        x = x + jnp.where(idx >= s, pltpu.roll(x, s, axis=1), 0.0)
        s *= 2
    return x


def _halves(a2):
    return a2[:, :HEAD_DIM], a2[:, HEAD_DIM:]


def _pair(a0, a1):
    return jnp.concatenate([a0, a1], axis=1)


def _block_diag(a2):
    a0, a1 = _halves(a2)
    z = jnp.zeros_like(a0)
    return jnp.concatenate([_pair(a0, z), _pair(z, a1)], axis=0)


def _pair_head_norm(x2, g2):
    x0, x1 = _halves(x2)
    return _pair(_normalize(x0), _normalize(x1)) * g2


def _rotary_pair(t2, cos, sin):
    t0, t1 = _halves(t2)
    return _pair(t0 * cos + pltpu.roll(t0, HEAD_DIM // 2, axis=1) * sin,
                 t1 * cos + pltpu.roll(t1, HEAD_DIM // 2, axis=1) * sin)


def _even_kernel(x_ref, g_ref, wm_ref, wr_ref, wgc_ref, wgr_ref, gbc_ref, gbr_ref, cw_ref, cb_ref,
                 cos_ref, sin_ref, mng_ref, rng_ref, wout_ref, o_ref,
                 zbuf, qkbuf, mixbuf, c_state, n_state, m_state, r_state, *, ret_log_decay):
    i = pl.program_id(1)
    ts = x_ref.shape[0]
    scale = HEAD_DIM ** -0.5
    n_chunks = ts // CHUNK

    @pl.when(i == 0)
    def _():
        zbuf[0:QK_HALO, :] = jnp.zeros((QK_HALO, zbuf.shape[1]), F32)
        c_state[...] = jnp.zeros(c_state.shape, F32)
        n_state[...] = jnp.zeros(n_state.shape, F32)
        m_state[...] = jnp.full(m_state.shape, NEG, F32)
        r_state[...] = jnp.zeros(r_state.shape, F32)

    x = x_ref[...]
    h = _rmsnorm(x, g_ref[...]).astype(BF16)
    width = N_HEADS * HEAD_DIM
    for pp in range(2 * N_PAIRS):
        w_ref = (wm_ref, wr_ref)[pp // N_PAIRS]
        for j in range(PAIR_COLS // PAIR_WIDTH):
            src = j * width + (pp % N_PAIRS) * PAIR_WIDTH
            dst = pp * PAIR_COLS + j * PAIR_WIDTH
            zbuf[QK_HALO:QK_HALO + ts, dst:dst + PAIR_WIDTH] = jnp.dot(
                h, w_ref[:, src:src + PAIR_WIDTH], preferred_element_type=F32)

    gate_c = jnp.dot(h, wgc_ref[...], preferred_element_type=F32) + gbc_ref[...]
    gate_r = _dot_nt(wgr_ref[...], h) + gbr_ref[...]
    ig_c = gate_c[:, 0:N_HEADS]
    b_c = _chunk_cumsum_rows(_log_sigmoid(gate_c), CHUNK)[:, N_HEADS:2 * N_HEADS]
    ig_r = gate_r[0:N_HEADS, :]
    b_r = _chunk_cumsum_lanes(_log_sigmoid(gate_r), CHUNK)[N_HEADS:2 * N_HEADS, :]

    row_i = lax.broadcasted_iota(jnp.int32, (CHUNK, CHUNK), 0)
    col_j = lax.broadcasted_iota(jnp.int32, (CHUNK, CHUNK), 1)
    causal = col_j <= row_i
    rel = (row_i - col_j).astype(F32)
    pos_c = lax.broadcasted_iota(jnp.int32, (CHUNK, 1), 0).astype(F32)
    blk_r = lax.broadcasted_iota(jnp.int32, (PAIR_WIDTH, PAIR_WIDTH), 0) // HEAD_DIM
    blk_c = lax.broadcasted_iota(jnp.int32, (PAIR_WIDTH, PAIR_WIDTH), 1) // HEAD_DIM
    same_head = blk_r == blk_c
    ones_row = jnp.ones((1, HEAD_DIM), F32)

    for p in range(N_PAIRS):
        base = p * PAIR_COLS
        qk_cols = slice(p * 2 * PAIR_WIDTH, (p + 1) * 2 * PAIR_WIDTH)
        q_cols = slice(p * PAIR_WIDTH, (p + 1) * PAIR_WIDTH)
        k_cols = slice(width + p * PAIR_WIDTH, width + (p + 1) * PAIR_WIDTH)
        conv = jnp.broadcast_to(_pair(cb_ref[:, q_cols], cb_ref[:, k_cols]), (ts, 2 * PAIR_WIDTH))
        for k in range(QK_CONV):
            start = QK_HALO - (QK_CONV - 1) + k
            w_k = _pair(cw_ref[k:k + 1, q_cols], cw_ref[k:k + 1, k_cols])
            conv = conv + w_k * zbuf[start:start + ts, base:base + 2 * PAIR_WIDTH]
        qkbuf[:, qk_cols] = _silu(conv)

    def mlstm_a(p, c, st):
        base = p * PAIR_COLS
        rows = slice(c * CHUNK, (c + 1) * CHUNK)
        zrows = slice(QK_HALO + c * CHUNK, QK_HALO + (c + 1) * CHUNK)
        q2 = qkbuf[rows, p * 2 * PAIR_WIDTH: p * 2 * PAIR_WIDTH + PAIR_WIDTH]
        k2 = qkbuf[rows, p * 2 * PAIR_WIDTH + PAIR_WIDTH: (p + 1) * 2 * PAIR_WIDTH] * scale
        v2 = zbuf[zrows, base + 2 * PAIR_WIDTH: base + 3 * PAIR_WIDTH]
        k_h = _halves(k2)
        log_d, m, inter, a_end, s_old, s_new, m_new = [], [], [], [], [], [], []
        for e in range(2):
            hd = 2 * p + e
            m_prev = st["m"][e]
            bc = b_c[rows, hd:hd + 1]
            ic = ig_c[rows, hd:hd + 1]
            br = b_r[hd:hd + 1, rows]
            ir = ig_r[hd:hd + 1, rows]
            ld = jnp.where(causal, bc - br + ir, NEG)
            m_intra = jnp.max(ld, axis=1, keepdims=True)
            total = bc[CHUNK - 1:CHUNK, :]
            w_end = total - bc + ic
            m_loc = jnp.max(w_end, axis=0, keepdims=True)
            log_inter = bc + m_prev
            m_e = jnp.maximum(m_intra, log_inter)
            mn = jnp.maximum(total + m_prev, m_loc)
            log_d.append(ld - m_e)
            m.append(m_e)
            inter.append(jnp.exp(log_inter - m_e))
            a_end.append(jnp.exp(w_end - m_loc))
            s_old.append(jnp.exp(total + m_prev - mn))
            s_new.append(jnp.exp(m_loc - mn))
            m_new.append(mn)
        ak2 = _pair(a_end[0] * k_h[0], a_end[1] * k_h[1])
        return dict(rows=rows, zrows=zrows, q2=q2, v2=v2, m=m, inter=inter,
                    decay=jnp.exp(_pair(log_d[0], log_d[1])),
                    qk=_dot_nt(q2, _block_diag(k2)),
                    q_state=_dot(q2, st["c"]),
                    kv=_dot_tn(ak2, v2),
                    n_sum=jnp.sum(ak2, axis=0, keepdims=True),
                    s_old2=_pair(s_old[0] * ones_row, s_old[1] * ones_row),
                    s_new2=_pair(s_new[0] * ones_row, s_new[1] * ones_row),
                    m_new=m_new)

    def mlstm_c(p, st, t):
        base = p * PAIR_COLS
        og2 = zbuf[t["zrows"], base + 3 * PAIR_WIDTH: base + 4 * PAIR_WIDTH]
        s_h = _halves(t["s2"])
        q_h = _halves(t["q2"])
        n_h = _halves(st["n"])
        hm = []
        for e in range(2):
            num = _halves(t["sv"])[e] + t["inter"][e] * _halves(t["q_state"])[e]
            den = (jnp.sum(s_h[e], axis=1, keepdims=True)
                   + t["inter"][e] * jnp.sum(q_h[e] * n_h[e], axis=1, keepdims=True))
            hm.append(num / jnp.maximum(jnp.abs(den), jnp.exp(-t["m"][e])))
        g2 = _pair(mng_ref[2 * p:2 * p + 1, :], mng_ref[2 * p + 1:2 * p + 2, :])
        hm2 = _pair_head_norm(_sigmoid(og2) * _pair(hm[0], hm[1]), g2)
        mixbuf[t["rows"], p * PAIR_WIDTH:(p + 1) * PAIR_WIDTH] = hm2.astype(BF16)
        st["c"] = t["s_old2"] * st["c"] + t["s_new2"] * jnp.where(same_head, t["kv"], 0.0)
        st["n"] = t["s_old2"] * st["n"] + t["s_new2"] * t["n_sum"]
        st["m"] = t["m_new"]

    def ret_a(p, c, st):
        base = (N_PAIRS + p) * PAIR_COLS
        rows = slice(c * CHUNK, (c + 1) * CHUNK)
        zrows = slice(QK_HALO + c * CHUNK, QK_HALO + (c + 1) * CHUNK)
        cos = cos_ref[rows, :]
        sin = sin_ref[rows, :]
        q2 = _rotary_pair(zbuf[zrows, base:base + PAIR_WIDTH], cos, sin)
        k2 = _rotary_pair(zbuf[zrows, base + PAIR_WIDTH:base + 2 * PAIR_WIDTH], cos, sin) * scale
        v2 = zbuf[zrows, base + 2 * PAIR_WIDTH: base + 3 * PAIR_WIDTH]
        return dict(rows=rows, zrows=zrows, v2=v2, decay=st["decay2"],
                    qk=_dot_nt(q2, _block_diag(k2)),
                    q_state=_dot(q2, st["r"]),
                    kv=_dot_tn(st["w_end2"] * k2, v2))

    def ret_c(p, st, t):
        base = (N_PAIRS + p) * PAIR_COLS
        rg2 = zbuf[t["zrows"], base + 3 * PAIR_WIDTH: base + 4 * PAIR_WIDTH]
        o2 = t["sv"] + t["q_state"] * st["q_decay2"]
        g2 = _pair(rng_ref[2 * p:2 * p + 1, :], rng_ref[2 * p + 1:2 * p + 2, :])
        hr2 = _silu(rg2) * _pair_head_norm(o2, g2)
        mixbuf[t["rows"], (N_PAIRS + p) * PAIR_WIDTH:(N_PAIRS + p + 1) * PAIR_WIDTH] = hr2.astype(BF16)
        st["r"] = st["chunk_decay2"] * st["r"] + jnp.where(same_head, t["kv"], 0.0)

    def stage_b(t):
        t["s2"] = t["qk"] * t["decay"]
        t["sv"] = _dot(t["s2"], _block_diag(t["v2"]))

    lanes = []
    for p in range(N_PAIRS):
        lanes.append((functools.partial(mlstm_a, p), functools.partial(mlstm_c, p),
                      dict(c=c_state[p], n=n_state[p],
                           m=[m_state[2 * p + e][:, 0:1] for e in range(2)])))
    for p in range(N_PAIRS):
        decay, w_end_r, q_decay, chunk_decay = [], [], [], []
        for e in range(2):
            lg = ret_log_decay[2 * p + e]
            decay.append(jnp.where(causal, jnp.exp(lg * jnp.maximum(rel, 0.0)), 0.0))
            w_end_r.append(jnp.exp(lg * (CHUNK - 1.0 - pos_c)) * ones_row)
            q_decay.append(jnp.exp(lg * (pos_c + 1.0)) * ones_row)
            chunk_decay.append(math.exp(lg * CHUNK) * ones_row)
        lanes.append((functools.partial(ret_a, p), functools.partial(ret_c, p),
                      dict(r=r_state[p], decay2=_pair(*decay), w_end2=_pair(*w_end_r),
                           q_decay2=_pair(*q_decay), chunk_decay2=_pair(*chunk_decay))))

    for c in range(n_chunks):
        work = [stage_a(c, st) for stage_a, _, st in lanes]
        for t in work:
            stage_b(t)
        for (_, stage_c, st), t in zip(lanes, work):
            stage_c(st, t)

    for p in range(N_PAIRS):
        st = lanes[p][2]
        c_state[p] = st["c"]
        n_state[p] = st["n"]
        for e in range(2):
            m_state[2 * p + e] = st["m"][e] * ones_row
        r_state[p] = lanes[N_PAIRS + p][2]["r"]

    o_ref[...] = x + jnp.dot(mixbuf[...], wout_ref[...], preferred_element_type=F32)
    zbuf[0:QK_HALO, :] = zbuf[ts:ts + QK_HALO, :]


def _retention_log_decays():
    h = np.arange(N_HEADS, dtype=np.float32)
    lg = np.log(np.float32(1.0) - np.float32(2.0) ** (np.float32(-5.0) - h)).astype(np.float32)
    return tuple(float(v) for v in lg)


@functools.lru_cache(maxsize=None)
def _rotary_tables(seq):
    inv = np.float32(ROPE_BASE) ** (-np.arange(0, HEAD_DIM, 2, dtype=np.float32) / np.float32(HEAD_DIM))
    ang = np.arange(seq, dtype=np.float32)[:, None] * inv[None, :]
    cos = np.cos(ang).astype(np.float32)
    sin = np.sin(ang).astype(np.float32)
    return np.concatenate([cos, cos], axis=-1), np.concatenate([-sin, sin], axis=-1)


def _even_mixer(x2, batch, seq, norm_g, w_in, qk_conv_w, qk_conv_b, i_bias, f_bias,
                mlstm_norm_g, ret_norm_g, w_out):
    n_tok, d = x2.shape
    width = N_HEADS * HEAD_DIM
    ts = SEQ_TILE
    nt = seq // ts
    assert nt * ts == seq and ts % CHUNK == 0
    g0 = 4 * width
    g1 = g0 + 2 * N_HEADS
    w_m = w_in[:, :g0].astype(BF16)
    w_r = w_in[:, g1:].astype(BF16)
    w_gate = w_in[:, g0:g1].astype(BF16)
    w_gate_c = jnp.pad(w_gate, ((0, 0), (0, HEAD_DIM - 2 * N_HEADS)))
    w_gate_r = w_gate.T
    gate_b = jnp.concatenate([i_bias, f_bias]).astype(F32)
    gate_b_c = jnp.pad(gate_b, (0, HEAD_DIM - 2 * N_HEADS)).reshape(1, HEAD_DIM)
    gate_b_r = gate_b.reshape(2 * N_HEADS, 1)
    cos_t, sin_t = _rotary_tables(seq)

    row = lambda b, i: (b * nt + i, 0)
    seq_row = lambda b, i: (i, 0)
    c2 = lambda b, i: (0, 0)
    return pl.pallas_call(
        functools.partial(_even_kernel, ret_log_decay=_retention_log_decays()),
        grid=(batch, nt),
        in_specs=[
            pl.BlockSpec((ts, d), row),
            pl.BlockSpec((1, d), c2),
            pl.BlockSpec(w_m.shape, c2),
            pl.BlockSpec(w_r.shape, c2),
            pl.BlockSpec(w_gate_c.shape, c2),
            pl.BlockSpec(w_gate_r.shape, c2),
            pl.BlockSpec(gate_b_c.shape, c2),
            pl.BlockSpec(gate_b_r.shape, c2),
            pl.BlockSpec(qk_conv_w.shape, c2),
            pl.BlockSpec((1, 2 * width), c2),
            pl.BlockSpec((ts, HEAD_DIM), seq_row),
            pl.BlockSpec((ts, HEAD_DIM), seq_row),
            pl.BlockSpec(mlstm_norm_g.shape, c2),
            pl.BlockSpec(ret_norm_g.shape, c2),
            pl.BlockSpec(w_out.shape, c2),
        ],
        out_specs=pl.BlockSpec((ts, d), row),
        out_shape=jax.ShapeDtypeStruct((n_tok, d), F32),
        scratch_shapes=[
            pltpu.VMEM((ts + QK_HALO, 8 * width), F32),
            pltpu.VMEM((ts, 2 * width), F32),
            pltpu.VMEM((ts, 2 * width), BF16),
            pltpu.VMEM((N_PAIRS, PAIR_WIDTH, PAIR_WIDTH), F32),
            pltpu.VMEM((N_PAIRS, 1, PAIR_WIDTH), F32),
            pltpu.VMEM((N_HEADS, 1, HEAD_DIM), F32),
            pltpu.VMEM((N_PAIRS, PAIR_WIDTH, PAIR_WIDTH), F32),
        ],
        compiler_params=pltpu.CompilerParams(
            dimension_semantics=("arbitrary", "arbitrary"), vmem_limit_bytes=VMEM_LIMIT_BYTES),
        name="even_mixer",
    )(x2, norm_g.reshape(1, d), w_m, w_r, w_gate_c, w_gate_r, gate_b_c, gate_b_r,
      qk_conv_w, qk_conv_b.reshape(1, 2 * width), jnp.asarray(cos_t), jnp.asarray(sin_t),
      mlstm_norm_g, ret_norm_g, w_out.astype(BF16))


def kernel(x, norm_mix_g, norm_ffn_g, final_norm_g, ev_w_in, ev_qk_conv_w, ev_qk_conv_b, ev_i_bias, ev_f_bias, ev_mlstm_norm_g, ev_ret_norm_g, ev_w_out, od_w_in, od_pool_w, od_pool_scale, od_dw_w, od_dw_b, od_conv_norm_g, od_conv_norm_b, od_w_out, ffn_w_gate, ffn_w_up, ffn_w_down):
    batch, seq, d = x.shape
    depth = norm_mix_g.shape[0]
    x2 = x.reshape(batch * seq, d)
    for layer in range(depth):
        j = layer // 2
        if layer % 2 == 0:
            x2 = _even_mixer(x2, batch, seq, norm_mix_g[layer], ev_w_in[j], ev_qk_conv_w[j],
                             ev_qk_conv_b[j], ev_i_bias[j], ev_f_bias[j], ev_mlstm_norm_g[j],
                             ev_ret_norm_g[j], ev_w_out[j])
        else:
            x2 = _odd_mixer(x2, batch, seq, norm_mix_g[layer], od_w_in[j], od_pool_w[j],
                            od_pool_scale[j], od_dw_w[j], od_dw_b[j], od_conv_norm_g[j],
                            od_conv_norm_b[j], od_w_out[j])
        x2 = _ffn(x2, layer, norm_ffn_g[layer], ffn_w_gate, ffn_w_up, ffn_w_down,
                  final_norm_g, apply_final_norm=(layer == depth - 1))
    return x2.reshape(batch, seq, d)
```

```python
import functools
import math

import numpy as np
import jax
import jax.numpy as jnp
from jax import lax
from jax.experimental import pallas as pl
from jax.experimental.pallas import tpu as pltpu

F32 = jnp.float32
BF16 = jnp.bfloat16

EPS = 1e-6
NEG = -1e30
CHUNK = 128
HEAD_DIM = 128
N_HEADS = 4
QK_CONV = 4
ROPE_BASE = 10000.0
POOL_WINDOWS = (2, 4, 8, 16)
POOL_GROUP = 128
DW_KERNEL = 31

SEQ_TILE = 512
FFN_TILE = 512
FF_CHUNK = 256
QK_HALO = 8
POOL_HALO = 16
DW_HALO = 32
DW_ROWS = 64
VMEM_LIMIT_BYTES = 56 * 1024 * 1024


def _rmsnorm(x, g):
    ms = jnp.mean(x * x, axis=-1, keepdims=True)
    return (x * lax.rsqrt(ms + EPS)) * g


def _normalize(x):
    xc = x - jnp.mean(x, axis=-1, keepdims=True)
    return xc * lax.rsqrt(jnp.mean(xc * xc, axis=-1, keepdims=True) + EPS)


def _sigmoid(x):
    return 1.0 / (1.0 + jnp.exp(-x))


def _silu(x):
    return x * _sigmoid(x)


def _log_sigmoid(x):
    return jnp.minimum(x, 0.0) - jnp.log(1.0 + jnp.exp(-jnp.abs(x)))


def _dot(a, b):
    return jnp.dot(a.astype(BF16), b.astype(BF16), preferred_element_type=F32)


def _dot_nt(a, b):
    return lax.dot_general(a.astype(BF16), b.astype(BF16), (((1,), (1,)), ((), ())),
                           preferred_element_type=F32)


def _dot_tn(a, b):
    return lax.dot_general(a.astype(BF16), b.astype(BF16), (((0,), (0,)), ((), ())),
                           preferred_element_type=F32)


def _ffn_kernel(x_ref, g_ref, wg_ref, wu_ref, wd_ref, fg_ref, o_ref, *, apply_final_norm):
    x = x_ref[...]
    h = _rmsnorm(x, g_ref[...]).astype(BF16)
    acc = x
    d_ff = wg_ref.shape[1]
    for c0 in range(0, d_ff, FF_CHUNK):
        cols = slice(c0, c0 + FF_CHUNK)
        gate = jnp.dot(h, wg_ref[:, cols].astype(BF16), preferred_element_type=F32)
        up = jnp.dot(h, wu_ref[:, cols].astype(BF16), preferred_element_type=F32)
        a = (_silu(gate) * up).astype(BF16)
        acc = acc + jnp.dot(a, wd_ref[cols, :].astype(BF16), preferred_element_type=F32)
    if apply_final_norm:
        acc = _rmsnorm(acc, fg_ref[...])
    o_ref[...] = acc


def _ffn(x2, layer, norm_g, w_gate, w_up, w_down, final_g, apply_final_norm):
    n_tok, d = x2.shape
    d_ff = w_gate.shape[2]
    assert d_ff % FF_CHUNK == 0 and n_tok % FFN_TILE == 0
    const2 = lambda i: (0, 0)
    pick = lambda i: (layer, 0, 0)
    return pl.pallas_call(
        functools.partial(_ffn_kernel, apply_final_norm=apply_final_norm),
        grid=(n_tok // FFN_TILE,),
        in_specs=[
            pl.BlockSpec((FFN_TILE, d), lambda i: (i, 0)),
            pl.BlockSpec((1, d), const2),
            pl.BlockSpec((None, d, d_ff), pick, pipeline_mode=pl.Buffered(1)),
            pl.BlockSpec((None, d, d_ff), pick, pipeline_mode=pl.Buffered(1)),
            pl.BlockSpec((None, d_ff, d), pick, pipeline_mode=pl.Buffered(1)),
            pl.BlockSpec((1, d), const2),
        ],
        out_specs=pl.BlockSpec((FFN_TILE, d), lambda i: (i, 0)),
        out_shape=jax.ShapeDtypeStruct((n_tok, d), F32),
        compiler_params=pltpu.CompilerParams(
            dimension_semantics=("arbitrary",), vmem_limit_bytes=VMEM_LIMIT_BYTES),
        name="ffn",
    )(x2, norm_g.reshape(1, d), w_gate, w_up, w_down, final_g.reshape(1, d))


def _interleave(main, fill):
    n = len(main)
    for s, fn in enumerate(main):
        fn()
        for g in fill[s * len(fill) // n:(s + 1) * len(fill) // n]:
            g()


def _odd_layer_kernel(x_ref, g_ref, win_ref, poolw_ref, pscale_ref, dww_ref, dwb_ref, cng_ref,
                      cnb_ref, wout_ref, fg_ref, wg_ref, wu_ref, wd_ref, final_g_ref, o_ref,
                      ubuf, ybuf, yrot, cbuf, pool_out, xmid,
                      *, pool_width, tiles_per_seq, n_tiles, apply_final_norm):
    s = pl.program_id(0)
    ts = x_ref.shape[0]
    conv_width = ybuf.shape[1]
    d_ff = wg_ref.shape[1]
    seq_tile = s % tiles_per_seq

    @pl.when(seq_tile == 0)
    def _():
        ubuf[0:POOL_HALO, :] = jnp.zeros((POOL_HALO, ubuf.shape[1]), F32)
        ybuf[0:DW_HALO, :] = jnp.zeros((DW_HALO, ybuf.shape[1]), F32)

    @pl.when(seq_tile != 0)
    def _():
        ubuf[0:POOL_HALO, :] = ubuf[ts:ts + POOL_HALO, :]
        ybuf[0:DW_HALO, :] = ybuf[ts:ts + DW_HALO, :]

    @pl.when(s >= 1)
    def _():
        yf = _normalize(cbuf[...]) * cng_ref[...] + cnb_ref[...]
        xmid[...] = pool_out[...] + _dot(_silu(yf), wout_ref[pool_width:, :])

    pool_st = {}

    def pool(g, w):
        cols = slice(g * POOL_GROUP, (g + 1) * POOL_GROUP)
        pos = (seq_tile * ts + lax.broadcasted_iota(jnp.int32, (ts, 1), 0)).astype(F32) + 1.0
        wsum = ubuf[:, cols]
        shift = 1
        while shift < w:
            wsum = wsum + pltpu.roll(wsum, shift, axis=0)
            shift *= 2
        cur = ubuf[POOL_HALO:POOL_HALO + ts, cols]
        pooled = wsum[POOL_HALO:, :] / jnp.minimum(pos, float(w)) - cur
        y_pool = _dot(pooled, poolw_ref[g]) * pscale_ref[:, cols]
        prev = x_ref[...] if g == 0 else pool_st["out"]
        pool_st["out"] = prev + _dot(y_pool, wout_ref[cols, :])
        if g == len(POOL_WINDOWS) - 1:
            pool_out[...] = pool_st["out"]

    @pl.when(s < n_tiles)
    def _():
        h = _rmsnorm(x_ref[...], g_ref[...]).astype(BF16)
        z = jnp.dot(h, win_ref[...], preferred_element_type=F32)
        ubuf[POOL_HALO:POOL_HALO + ts, :] = z[:, :pool_width]
        ga = z[:, pool_width:pool_width + conv_width]
        gb = z[:, pool_width + conv_width:]
        ybuf[DW_HALO:DW_HALO + ts, :] = ga * _sigmoid(gb)
        for g, w in enumerate(POOL_WINDOWS):
            pool(g, w)

    rot_rows = yrot.shape[1]

    def shifted_copy(r):
        yrot[r - 1] = ybuf[r:r + rot_rows, :]

    def conv(r0, c0):
        cols = slice(c0, c0 + 128)
        acc = jnp.broadcast_to(dwb_ref[:, cols], (DW_ROWS, 128))
        for k in range(DW_KERNEL):
            start = DW_HALO - (DW_KERNEL - 1) + k + r0
            src = ybuf if start % 8 == 0 else yrot.at[start % 8 - 1]
            aligned = start - start % 8
            acc = acc + dww_ref[k:k + 1, cols] * src[aligned:aligned + DW_ROWS, cols]
        cbuf[r0:r0 + DW_ROWS, cols] = acc

    vector_pieces = [functools.partial(shifted_copy, r) for r in range(1, 8)]
    vector_pieces += [functools.partial(conv, r0, c0) for r0 in range(0, ts, DW_ROWS)
                      for c0 in range(0, conv_width, 128)]

    st = {}

    def ffn_start():
        st["acc"] = xmid[...]
        st["h"] = _rmsnorm(st["acc"], fg_ref[...]).astype(BF16)

    def ffn_gate(c0):
        st["gate"] = jnp.dot(st["h"], wg_ref[:, c0:c0 + FF_CHUNK], preferred_element_type=F32)

    def ffn_up(c0):
        up = jnp.dot(st["h"], wu_ref[:, c0:c0 + FF_CHUNK], preferred_element_type=F32)
        st["a"] = (_silu(st["gate"]) * up).astype(BF16)

    def ffn_down(c0):
        st["acc"] = st["acc"] + jnp.dot(st["a"], wd_ref[c0:c0 + FF_CHUNK, :],
                                        preferred_element_type=F32)

    def ffn_finish():
        acc = st["acc"]
        if apply_final_norm:
            acc = _rmsnorm(acc, final_g_ref[...])
        o_ref[...] = acc

    matmul_pieces = [ffn_start]
    for c0 in range(0, d_ff, FF_CHUNK):
        matmul_pieces += [functools.partial(f, c0) for f in (ffn_gate, ffn_up, ffn_down)]
    matmul_pieces += [ffn_finish]

    @pl.when(s == 0)
    def _():
        for piece in vector_pieces:
            piece()

    @pl.when(s >= 1)
    def _():
        _interleave(matmul_pieces, vector_pieces)


def _odd_layer(x2, batch, seq, norm_g, w_in, pool_w, pool_scale, dw_w, dw_b, cn_g, cn_b, w_out,
               ffn_norm_g, w_gate, w_up, w_down, final_g, apply_final_norm):
    n_tok, d = x2.shape
    pool_width = pool_w.shape[0] * pool_w.shape[1]
    conv_width = dw_w.shape[1]
    d_ff = w_gate.shape[1]
    ts = SEQ_TILE
    nt = seq // ts
    n_tiles = batch * nt
    assert nt * ts == seq and ts % DW_ROWS == 0 and d_ff % FF_CHUNK == 0
    c2 = lambda s: (0, 0)
    c3 = lambda s: (0, 0, 0)
    resident = functools.partial(pl.BlockSpec, pipeline_mode=pl.Buffered(1))
    return pl.pallas_call(
        functools.partial(_odd_layer_kernel, pool_width=pool_width, tiles_per_seq=nt,
                          n_tiles=n_tiles, apply_final_norm=apply_final_norm),
        grid=(n_tiles + 1,),
        in_specs=[
            pl.BlockSpec((ts, d), lambda s: (jnp.minimum(s, n_tiles - 1), 0)),
            pl.BlockSpec((1, d), c2),
            resident(w_in.shape, c2),
            pl.BlockSpec(pool_w.shape, c3),
            pl.BlockSpec((1, pool_width), c2),
            pl.BlockSpec(dw_w.shape, c2),
            pl.BlockSpec((1, conv_width), c2),
            pl.BlockSpec((1, conv_width), c2),
            pl.BlockSpec((1, conv_width), c2),
            resident(w_out.shape, c2),
            pl.BlockSpec((1, d), c2),
            resident((d, d_ff), c2),
            resident((d, d_ff), c2),
            resident((d_ff, d), c2),
            pl.BlockSpec((1, d), c2),
        ],
        out_specs=pl.BlockSpec((ts, d), lambda s: (jnp.maximum(s - 1, 0), 0)),
        out_shape=jax.ShapeDtypeStruct((n_tok, d), F32),
        scratch_shapes=[
            pltpu.VMEM((ts + POOL_HALO, pool_width), F32),
            pltpu.VMEM((ts + DW_HALO, conv_width), F32),
            pltpu.VMEM((7, ts + DW_HALO - 8, conv_width), F32),
            pltpu.VMEM((ts, conv_width), F32),
            pltpu.VMEM((ts, d), F32),
            pltpu.VMEM((ts, d), F32),
        ],
        compiler_params=pltpu.CompilerParams(
            dimension_semantics=("arbitrary",), vmem_limit_bytes=VMEM_LIMIT_BYTES),
        name="odd_layer",
    )(x2, norm_g.reshape(1, d), w_in.astype(BF16), pool_w.astype(BF16),
      pool_scale.reshape(1, pool_width), dw_w, dw_b.reshape(1, conv_width),
      cn_g.reshape(1, conv_width), cn_b.reshape(1, conv_width), w_out.astype(BF16),
      ffn_norm_g.reshape(1, d), w_gate.astype(BF16), w_up.astype(BF16), w_down.astype(BF16),
      final_g.reshape(1, d))


PAIR_WIDTH = 2 * HEAD_DIM
N_PAIRS = N_HEADS // 2
PAIR_COLS = 4 * PAIR_WIDTH


def _chunk_cumsum_rows(x, n):
    idx = lax.broadcasted_iota(jnp.int32, x.shape, 0) % n
    s = 1
    while s < n:
        x = x + jnp.where(idx >= s, pltpu.roll(x, s, axis=0), 0.0)
        s *= 2
    return x


def _chunk_cumsum_lanes(x, n):
    idx = lax.broadcasted_iota(jnp.int32, x.shape, 1) % n
    s = 1
    while s < n:
        x = x + jnp.where(idx >= s, pltpu.roll(x, s, axis=1), 0.0)
        s *= 2
    return x


def _halves(a2):
    return a2[:, :HEAD_DIM], a2[:, HEAD_DIM:]


def _pair(a0, a1):
    return jnp.concatenate([a0, a1], axis=1)


def _block_diag(a2):
    a0, a1 = _halves(a2)
    z = jnp.zeros_like(a0)
    return jnp.concatenate([_pair(a0, z), _pair(z, a1)], axis=0)


def _pair_head_norm(x2, g2):
    x0, x1 = _halves(x2)
    return _pair(_normalize(x0), _normalize(x1)) * g2


def _rotary_pair(t2, cos, sin):
    t0, t1 = _halves(t2)
    return _pair(t0 * cos + pltpu.roll(t0, HEAD_DIM // 2, axis=1) * sin,
                 t1 * cos + pltpu.roll(t1, HEAD_DIM // 2, axis=1) * sin)


def _even_kernel(x_ref, g_ref, wm_ref, wr_ref, wgc_ref, wgr_ref, gbc_ref, gbr_ref, cw_ref, cb_ref,
                 cos_ref, sin_ref, mng_ref, rng_ref, wout_ref, o_ref,
                 zbuf, qkbuf, mixbuf, c_state, n_state, m_state, r_state, *, ret_log_decay):
    i = pl.program_id(1)
    ts = x_ref.shape[0]
    scale = HEAD_DIM ** -0.5
    n_chunks = ts // CHUNK

    @pl.when(i == 0)
    def _():
        zbuf[0:QK_HALO, :] = jnp.zeros((QK_HALO, zbuf.shape[1]), F32)
        c_state[...] = jnp.zeros(c_state.shape, F32)
        n_state[...] = jnp.zeros(n_state.shape, F32)
        m_state[...] = jnp.full(m_state.shape, NEG, F32)
        r_state[...] = jnp.zeros(r_state.shape, F32)

    x = x_ref[...]
    h = _rmsnorm(x, g_ref[...]).astype(BF16)
    width = N_HEADS * HEAD_DIM
    for pp in range(2 * N_PAIRS):
        w_ref = (wm_ref, wr_ref)[pp // N_PAIRS]
        for j in range(PAIR_COLS // PAIR_WIDTH):
            src = j * width + (pp % N_PAIRS) * PAIR_WIDTH
            dst = pp * PAIR_COLS + j * PAIR_WIDTH
            zbuf[QK_HALO:QK_HALO + ts, dst:dst + PAIR_WIDTH] = jnp.dot(
                h, w_ref[:, src:src + PAIR_WIDTH], preferred_element_type=F32)

    gate_c = jnp.dot(h, wgc_ref[...], preferred_element_type=F32) + gbc_ref[...]
    gate_r = _dot_nt(wgr_ref[...], h) + gbr_ref[...]
    ig_c = gate_c[:, 0:N_HEADS]
    b_c = _chunk_cumsum_rows(_log_sigmoid(gate_c), CHUNK)[:, N_HEADS:2 * N_HEADS]
    ig_r = gate_r[0:N_HEADS, :]
    b_r = _chunk_cumsum_lanes(_log_sigmoid(gate_r), CHUNK)[N_HEADS:2 * N_HEADS, :]

    row_i = lax.broadcasted_iota(jnp.int32, (CHUNK, CHUNK), 0)
    col_j = lax.broadcasted_iota(jnp.int32, (CHUNK, CHUNK), 1)
    causal = col_j <= row_i
    rel = (row_i - col_j).astype(F32)
    pos_c = lax.broadcasted_iota(jnp.int32, (CHUNK, 1), 0).astype(F32)
    blk_r = lax.broadcasted_iota(jnp.int32, (PAIR_WIDTH, PAIR_WIDTH), 0) // HEAD_DIM
    blk_c = lax.broadcasted_iota(jnp.int32, (PAIR_WIDTH, PAIR_WIDTH), 1) // HEAD_DIM
    same_head = blk_r == blk_c
    ones_row = jnp.ones((1, HEAD_DIM), F32)

    for p in range(N_PAIRS):
        base = p * PAIR_COLS
        qk_cols = slice(p * 2 * PAIR_WIDTH, (p + 1) * 2 * PAIR_WIDTH)
        q_cols = slice(p * PAIR_WIDTH, (p + 1) * PAIR_WIDTH)
        k_cols = slice(width + p * PAIR_WIDTH, width + (p + 1) * PAIR_WIDTH)
        conv = jnp.broadcast_to(_pair(cb_ref[:, q_cols], cb_ref[:, k_cols]), (ts, 2 * PAIR_WIDTH))
        for k in range(QK_CONV):
            start = QK_HALO - (QK_CONV - 1) + k
            w_k = _pair(cw_ref[k:k + 1, q_cols], cw_ref[k:k + 1, k_cols])
            conv = conv + w_k * zbuf[start:start + ts, base:base + 2 * PAIR_WIDTH]
        qkbuf[:, qk_cols] = _silu(conv)

    def mlstm_a(p, c, st):
        base = p * PAIR_COLS
        rows = slice(c * CHUNK, (c + 1) * CHUNK)
        zrows = slice(QK_HALO + c * CHUNK, QK_HALO + (c + 1) * CHUNK)
        q2 = qkbuf[rows, p * 2 * PAIR_WIDTH: p * 2 * PAIR_WIDTH + PAIR_WIDTH]
        k2 = qkbuf[rows, p * 2 * PAIR_WIDTH + PAIR_WIDTH: (p + 1) * 2 * PAIR_WIDTH] * scale
        v2 = zbuf[zrows, base + 2 * PAIR_WIDTH: base + 3 * PAIR_WIDTH]
        k_h = _halves(k2)
        log_d, m, inter, a_end, s_old, s_new, m_new = [], [], [], [], [], [], []
        for e in range(2):
            hd = 2 * p + e
            m_prev = st["m"][e]
            bc = b_c[rows, hd:hd + 1]
            ic = ig_c[rows, hd:hd + 1]
            br = b_r[hd:hd + 1, rows]
            ir = ig_r[hd:hd + 1, rows]
            ld = jnp.where(causal, bc - br + ir, NEG)
            m_intra = jnp.max(ld, axis=1, keepdims=True)
            total = bc[CHUNK - 1:CHUNK, :]
            w_end = total - bc + ic
            m_loc = jnp.max(w_end, axis=0, keepdims=True)
            log_inter = bc + m_prev
            m_e = jnp.maximum(m_intra, log_inter)
            mn = jnp.maximum(total + m_prev, m_loc)
            log_d.append(ld - m_e)
            m.append(m_e)
            inter.append(jnp.exp(log_inter - m_e))
            a_end.append(jnp.exp(w_end - m_loc))
            s_old.append(jnp.exp(total + m_prev - mn))
            s_new.append(jnp.exp(m_loc - mn))
            m_new.append(mn)
        ak2 = _pair(a_end[0] * k_h[0], a_end[1] * k_h[1])
        return dict(rows=rows, zrows=zrows, q2=q2, v2=v2, m=m, inter=inter,
                    decay=jnp.exp(_pair(log_d[0], log_d[1])),
                    qk=_dot_nt(q2, _block_diag(k2)),
                    q_state=_dot(q2, st["c"]),
                    kv=_dot_tn(ak2, v2),
                    n_sum=jnp.sum(ak2, axis=0, keepdims=True),
                    s_old2=_pair(s_old[0] * ones_row, s_old[1] * ones_row),
                    s_new2=_pair(s_new[0] * ones_row, s_new[1] * ones_row),
                    m_new=m_new)

    def mlstm_c(p, st, t):
        base = p * PAIR_COLS
        og2 = zbuf[t["zrows"], base + 3 * PAIR_WIDTH: base + 4 * PAIR_WIDTH]
        s_h = _halves(t["s2"])
        q_h = _halves(t["q2"])
        n_h = _halves(st["n"])
        hm = []
        for e in range(2):
            num = _halves(t["sv"])[e] + t["inter"][e] * _halves(t["q_state"])[e]
            den = (jnp.sum(s_h[e], axis=1, keepdims=True)
                   + t["inter"][e] * jnp.sum(q_h[e] * n_h[e], axis=1, keepdims=True))
            hm.append(num / jnp.maximum(jnp.abs(den), jnp.exp(-t["m"][e])))
        g2 = _pair(mng_ref[2 * p:2 * p + 1, :], mng_ref[2 * p + 1:2 * p + 2, :])
        hm2 = _pair_head_norm(_sigmoid(og2) * _pair(hm[0], hm[1]), g2)
        mixbuf[t["rows"], p * PAIR_WIDTH:(p + 1) * PAIR_WIDTH] = hm2.astype(BF16)
        st["c"] = t["s_old2"] * st["c"] + t["s_new2"] * jnp.where(same_head, t["kv"], 0.0)
        st["n"] = t["s_old2"] * st["n"] + t["s_new2"] * t["n_sum"]
        st["m"] = t["m_new"]

    def ret_a(p, c, st):
        base = (N_PAIRS + p) * PAIR_COLS
        rows = slice(c * CHUNK, (c + 1) * CHUNK)
        zrows = slice(QK_HALO + c * CHUNK, QK_HALO + (c + 1) * CHUNK)
        cos = cos_ref[rows, :]
        sin = sin_ref[rows, :]
        q2 = _rotary_pair(zbuf[zrows, base:base + PAIR_WIDTH], cos, sin)
        k2 = _rotary_pair(zbuf[zrows, base + PAIR_WIDTH:base + 2 * PAIR_WIDTH], cos, sin) * scale
        v2 = zbuf[zrows, base + 2 * PAIR_WIDTH: base + 3 * PAIR_WIDTH]
        return dict(rows=rows, zrows=zrows, v2=v2, decay=st["decay2"],
                    qk=_dot_nt(q2, _block_diag(k2)),
                    q_state=_dot(q2, st["r"]),
                    kv=_dot_tn(st["w_end2"] * k2, v2))

    def ret_c(p, st, t):
        base = (N_PAIRS + p) * PAIR_COLS
        rg2 = zbuf[t["zrows"], base + 3 * PAIR_WIDTH: base + 4 * PAIR_WIDTH]
        o2 = t["sv"] + t["q_state"] * st["q_decay2"]
        g2 = _pair(rng_ref[2 * p:2 * p + 1, :], rng_ref[2 * p + 1:2 * p + 2, :])
        hr2 = _silu(rg2) * _pair_head_norm(o2, g2)
        mixbuf[t["rows"], (N_PAIRS + p) * PAIR_WIDTH:(N_PAIRS + p + 1) * PAIR_WIDTH] = hr2.astype(BF16)
        st["r"] = st["chunk_decay2"] * st["r"] + jnp.where(same_head, t["kv"], 0.0)

    def stage_b(t):
        t["s2"] = t["qk"] * t["decay"]
        t["sv"] = _dot(t["s2"], _block_diag(t["v2"]))

    lanes = []
    for p in range(N_PAIRS):
        lanes.append((functools.partial(mlstm_a, p), functools.partial(mlstm_c, p),
                      dict(c=c_state[p], n=n_state[p],
                           m=[m_state[2 * p + e][:, 0:1] for e in range(2)])))
    for p in range(N_PAIRS):
        decay, w_end_r, q_decay, chunk_decay = [], [], [], []
        for e in range(2):
            lg = ret_log_decay[2 * p + e]
            decay.append(jnp.where(causal, jnp.exp(lg * jnp.maximum(rel, 0.0)), 0.0))
            w_end_r.append(jnp.exp(lg * (CHUNK - 1.0 - pos_c)) * ones_row)
            q_decay.append(jnp.exp(lg * (pos_c + 1.0)) * ones_row)
            chunk_decay.append(math.exp(lg * CHUNK) * ones_row)
        lanes.append((functools.partial(ret_a, p), functools.partial(ret_c, p),
                      dict(r=r_state[p], decay2=_pair(*decay), w_end2=_pair(*w_end_r),
                           q_decay2=_pair(*q_decay), chunk_decay2=_pair(*chunk_decay))))

    for c in range(n_chunks):
        work = [stage_a(c, st) for stage_a, _, st in lanes]
        for t in work:
            stage_b(t)
        for (_, stage_c, st), t in zip(lanes, work):
            stage_c(st, t)

    for p in range(N_PAIRS):
        st = lanes[p][2]
        c_state[p] = st["c"]
        n_state[p] = st["n"]
        for e in range(2):
            m_state[2 * p + e] = st["m"][e] * ones_row
        r_state[p] = lanes[N_PAIRS + p][2]["r"]

    o_ref[...] = x + jnp.dot(mixbuf[...], wout_ref[...], preferred_element_type=F32)
    zbuf[0:QK_HALO, :] = zbuf[ts:ts + QK_HALO, :]


def _retention_log_decays():
    h = np.arange(N_HEADS, dtype=np.float32)
    lg = np.log(np.float32(1.0) - np.float32(2.0) ** (np.float32(-5.0) - h)).astype(np.float32)
    return tuple(float(v) for v in lg)


@functools.lru_cache(maxsize=None)
def _rotary_tables(seq):
    inv = np.float32(ROPE_BASE) ** (-np.arange(0, HEAD_DIM, 2, dtype=np.float32) / np.float32(HEAD_DIM))
    ang = np.arange(seq, dtype=np.float32)[:, None] * inv[None, :]
    cos = np.cos(ang).astype(np.float32)
    sin = np.sin(ang).astype(np.float32)
    return np.concatenate([cos, cos], axis=-1), np.concatenate([-sin, sin], axis=-1)


def _even_mixer(x2, batch, seq, norm_g, w_in, qk_conv_w, qk_conv_b, i_bias, f_bias,
                mlstm_norm_g, ret_norm_g, w_out):
    n_tok, d = x2.shape
    width = N_HEADS * HEAD_DIM
    ts = SEQ_TILE
    nt = seq // ts
    assert nt * ts == seq and ts % CHUNK == 0
    g0 = 4 * width
    g1 = g0 + 2 * N_HEADS
    w_m = w_in[:, :g0].astype(BF16)
    w_r = w_in[:, g1:].astype(BF16)
    w_gate = w_in[:, g0:g1].astype(BF16)
    w_gate_c = jnp.pad(w_gate, ((0, 0), (0, HEAD_DIM - 2 * N_HEADS)))
    w_gate_r = w_gate.T
    gate_b = jnp.concatenate([i_bias, f_bias]).astype(F32)
    gate_b_c = jnp.pad(gate_b, (0, HEAD_DIM - 2 * N_HEADS)).reshape(1, HEAD_DIM)
    gate_b_r = gate_b.reshape(2 * N_HEADS, 1)
    cos_t, sin_t = _rotary_tables(seq)

    row = lambda b, i: (b * nt + i, 0)
    seq_row = lambda b, i: (i, 0)
    c2 = lambda b, i: (0, 0)
    return pl.pallas_call(
        functools.partial(_even_kernel, ret_log_decay=_retention_log_decays()),
        grid=(batch, nt),
        in_specs=[
            pl.BlockSpec((ts, d), row),
            pl.BlockSpec((1, d), c2),
            pl.BlockSpec(w_m.shape, c2),
            pl.BlockSpec(w_r.shape, c2),
            pl.BlockSpec(w_gate_c.shape, c2),
            pl.BlockSpec(w_gate_r.shape, c2),
            pl.BlockSpec(gate_b_c.shape, c2),
            pl.BlockSpec(gate_b_r.shape, c2),
            pl.BlockSpec(qk_conv_w.shape, c2),
            pl.BlockSpec((1, 2 * width), c2),
            pl.BlockSpec((ts, HEAD_DIM), seq_row),
            pl.BlockSpec((ts, HEAD_DIM), seq_row),
            pl.BlockSpec(mlstm_norm_g.shape, c2),
            pl.BlockSpec(ret_norm_g.shape, c2),
            pl.BlockSpec(w_out.shape, c2),
        ],
        out_specs=pl.BlockSpec((ts, d), row),
        out_shape=jax.ShapeDtypeStruct((n_tok, d), F32),
        scratch_shapes=[
            pltpu.VMEM((ts + QK_HALO, 8 * width), F32),
            pltpu.VMEM((ts, 2 * width), F32),
            pltpu.VMEM((ts, 2 * width), BF16),
            pltpu.VMEM((N_PAIRS, PAIR_WIDTH, PAIR_WIDTH), F32),
            pltpu.VMEM((N_PAIRS, 1, PAIR_WIDTH), F32),
            pltpu.VMEM((N_HEADS, 1, HEAD_DIM), F32),
            pltpu.VMEM((N_PAIRS, PAIR_WIDTH, PAIR_WIDTH), F32),
        ],
        compiler_params=pltpu.CompilerParams(
            dimension_semantics=("arbitrary", "arbitrary"), vmem_limit_bytes=VMEM_LIMIT_BYTES),
        name="even_mixer",
    )(x2, norm_g.reshape(1, d), w_m, w_r, w_gate_c, w_gate_r, gate_b_c, gate_b_r,
      qk_conv_w, qk_conv_b.reshape(1, 2 * width), jnp.asarray(cos_t), jnp.asarray(sin_t),
      mlstm_norm_g, ret_norm_g, w_out.astype(BF16))


def kernel(x, norm_mix_g, norm_ffn_g, final_norm_g, ev_w_in, ev_qk_conv_w, ev_qk_conv_b, ev_i_bias, ev_f_bias, ev_mlstm_norm_g, ev_ret_norm_g, ev_w_out, od_w_in, od_pool_w, od_pool_scale, od_dw_w, od_dw_b, od_conv_norm_g, od_conv_norm_b, od_w_out, ffn_w_gate, ffn_w_up, ffn_w_down):
    batch, seq, d = x.shape
    depth = norm_mix_g.shape[0]
    x2 = x.reshape(batch * seq, d)
    for layer in range(depth):
        j = layer // 2
        if layer % 2 == 0:
            x2 = _even_mixer(x2, batch, seq, norm_mix_g[layer], ev_w_in[j], ev_qk_conv_w[j],
                             ev_qk_conv_b[j], ev_i_bias[j], ev_f_bias[j], ev_mlstm_norm_g[j],
                             ev_ret_norm_g[j], ev_w_out[j])
            x2 = _ffn(x2, layer, norm_ffn_g[layer], ffn_w_gate, ffn_w_up, ffn_w_down,
                      final_norm_g, apply_final_norm=(layer == depth - 1))
        else:
            x2 = _odd_layer(x2, batch, seq, norm_mix_g[layer], od_w_in[j], od_pool_w[j],
                            od_pool_scale[j], od_dw_w[j], od_dw_b[j], od_conv_norm_g[j],
                            od_conv_norm_b[j], od_w_out[j], norm_ffn_g[layer], ffn_w_gate[layer],
                            ffn_w_up[layer], ffn_w_down[layer], final_norm_g,
                            apply_final_norm=(layer == depth - 1))
    return x2.reshape(batch, seq, d)
```

```python
import functools
import math

import numpy as np
import jax
import jax.numpy as jnp
from jax import lax
from jax.experimental import pallas as pl
from jax.experimental.pallas import tpu as pltpu

F32 = jnp.float32
BF16 = jnp.bfloat16

EPS = 1e-6
NEG = -1e30
CHUNK = 128
HEAD_DIM = 128
N_HEADS = 4
QK_CONV = 4
ROPE_BASE = 10000.0
POOL_WINDOWS = (2, 4, 8, 16)
POOL_GROUP = 128
DW_KERNEL = 31

SEQ_TILE = 512
FFN_TILE = 512
FF_CHUNK = 256
QK_HALO = 8
POOL_HALO = 16
DW_HALO = 32
DW_ROWS = 64
VMEM_LIMIT_BYTES = 56 * 1024 * 1024


def _rmsnorm(x, g):
    ms = jnp.mean(x * x, axis=-1, keepdims=True)
    return (x * lax.rsqrt(ms + EPS)) * g


def _normalize(x):
    xc = x - jnp.mean(x, axis=-1, keepdims=True)
    return xc * lax.rsqrt(jnp.mean(xc * xc, axis=-1, keepdims=True) + EPS)


def _sigmoid(x):
    return 1.0 / (1.0 + jnp.exp(-x))


def _silu(x):
    return x * _sigmoid(x)


def _log_sigmoid(x):
    return jnp.minimum(x, 0.0) - jnp.log(1.0 + jnp.exp(-jnp.abs(x)))


def _dot(a, b):
    return jnp.dot(a.astype(BF16), b.astype(BF16), preferred_element_type=F32)


def _dot_nt(a, b):
    return lax.dot_general(a.astype(BF16), b.astype(BF16), (((1,), (1,)), ((), ())),
                           preferred_element_type=F32)


def _dot_tn(a, b):
    return lax.dot_general(a.astype(BF16), b.astype(BF16), (((0,), (0,)), ((), ())),
                           preferred_element_type=F32)


def _ffn_kernel(x_ref, g_ref, wg_ref, wu_ref, wd_ref, fg_ref, o_ref, *, apply_final_norm):
    x = x_ref[...]
    h = _rmsnorm(x, g_ref[...]).astype(BF16)
    acc = x
    d_ff = wg_ref.shape[1]
    for c0 in range(0, d_ff, FF_CHUNK):
        cols = slice(c0, c0 + FF_CHUNK)
        gate = jnp.dot(h, wg_ref[:, cols].astype(BF16), preferred_element_type=F32)
        up = jnp.dot(h, wu_ref[:, cols].astype(BF16), preferred_element_type=F32)
        a = (_silu(gate) * up).astype(BF16)
        acc = acc + jnp.dot(a, wd_ref[cols, :].astype(BF16), preferred_element_type=F32)
    if apply_final_norm:
        acc = _rmsnorm(acc, fg_ref[...])
    o_ref[...] = acc


def _ffn(x2, layer, norm_g, w_gate, w_up, w_down, final_g, apply_final_norm):
    n_tok, d = x2.shape
    d_ff = w_gate.shape[2]
    assert d_ff % FF_CHUNK == 0 and n_tok % FFN_TILE == 0
    const2 = lambda i: (0, 0)
    pick = lambda i: (layer, 0, 0)
    return pl.pallas_call(
        functools.partial(_ffn_kernel, apply_final_norm=apply_final_norm),
        grid=(n_tok // FFN_TILE,),
        in_specs=[
            pl.BlockSpec((FFN_TILE, d), lambda i: (i, 0)),
            pl.BlockSpec((1, d), const2),
            pl.BlockSpec((None, d, d_ff), pick, pipeline_mode=pl.Buffered(1)),
            pl.BlockSpec((None, d, d_ff), pick, pipeline_mode=pl.Buffered(1)),
            pl.BlockSpec((None, d_ff, d), pick, pipeline_mode=pl.Buffered(1)),
            pl.BlockSpec((1, d), const2),
        ],
        out_specs=pl.BlockSpec((FFN_TILE, d), lambda i: (i, 0)),
        out_shape=jax.ShapeDtypeStruct((n_tok, d), F32),
        compiler_params=pltpu.CompilerParams(
            dimension_semantics=("arbitrary",), vmem_limit_bytes=VMEM_LIMIT_BYTES),
        name="ffn",
    )(x2, norm_g.reshape(1, d), w_gate, w_up, w_down, final_g.reshape(1, d))


def _interleave(main, fill):
    n = len(main)
    for s, fn in enumerate(main):
        fn()
        for g in fill[s * len(fill) // n:(s + 1) * len(fill) // n]:
            g()


def _odd_layer_kernel(x_ref, g_ref, win_ref, poolw_ref, pscale_ref, dww_ref, dwb_ref, cng_ref,
                      cnb_ref, wout_ref, fg_ref, wg_ref, wu_ref, wd_ref, final_g_ref, o_ref,
                      ubuf, ybuf, yrot, cbuf, pool_out, xmid,
                      *, pool_width, tiles_per_seq, n_tiles, apply_final_norm):
    s = pl.program_id(0)
    ts = x_ref.shape[0]
    conv_width = ybuf.shape[1]
    d_ff = wg_ref.shape[1]
    seq_tile = s % tiles_per_seq

    @pl.when(seq_tile == 0)
    def _():
        ubuf[0:POOL_HALO, :] = jnp.zeros((POOL_HALO, ubuf.shape[1]), F32)
        ybuf[0:DW_HALO, :] = jnp.zeros((DW_HALO, ybuf.shape[1]), F32)

    @pl.when(seq_tile != 0)
    def _():
        ubuf[0:POOL_HALO, :] = ubuf[ts:ts + POOL_HALO, :]
        ybuf[0:DW_HALO, :] = ybuf[ts:ts + DW_HALO, :]

    @pl.when(s >= 1)
    def _():
        yf = _normalize(cbuf[...]) * cng_ref[...] + cnb_ref[...]
        xmid[...] = pool_out[...] + _dot(_silu(yf), wout_ref[pool_width:, :])

    pool_st = {}

    def pool(g, w):
        cols = slice(g * POOL_GROUP, (g + 1) * POOL_GROUP)
        pos = (seq_tile * ts + lax.broadcasted_iota(jnp.int32, (ts, 1), 0)).astype(F32) + 1.0
        wsum = ubuf[:, cols]
        shift = 1
        while shift < w:
            wsum = wsum + pltpu.roll(wsum, shift, axis=0)
            shift *= 2
        cur = ubuf[POOL_HALO:POOL_HALO + ts, cols]
        pooled = wsum[POOL_HALO:, :] / jnp.minimum(pos, float(w)) - cur
        y_pool = _dot(pooled, poolw_ref[g]) * pscale_ref[:, cols]
        prev = x_ref[...] if g == 0 else pool_st["out"]
        pool_st["out"] = prev + _dot(y_pool, wout_ref[cols, :])
        if g == len(POOL_WINDOWS) - 1:
            pool_out[...] = pool_st["out"]

    @pl.when(s < n_tiles)
    def _():
        h = _rmsnorm(x_ref[...], g_ref[...]).astype(BF16)
        z = jnp.dot(h, win_ref[...], preferred_element_type=F32)
        ubuf[POOL_HALO:POOL_HALO + ts, :] = z[:, :pool_width]
        ga = z[:, pool_width:pool_width + conv_width]
        gb = z[:, pool_width + conv_width:]
        ybuf[DW_HALO:DW_HALO + ts, :] = ga * _sigmoid(gb)
        for g, w in enumerate(POOL_WINDOWS):
            pool(g, w)

    rot_rows = yrot.shape[1]

    def shifted_copy(r):
        yrot[r - 1] = ybuf[r:r + rot_rows, :]

    def conv(r0, c0):
        cols = slice(c0, c0 + 128)
        acc = jnp.broadcast_to(dwb_ref[:, cols], (DW_ROWS, 128))
        for k in range(DW_KERNEL):
            start = DW_HALO - (DW_KERNEL - 1) + k + r0
            src = ybuf if start % 8 == 0 else yrot.at[start % 8 - 1]
            aligned = start - start % 8
            acc = acc + dww_ref[k:k + 1, cols] * src[aligned:aligned + DW_ROWS, cols]
        cbuf[r0:r0 + DW_ROWS, cols] = acc

    vector_pieces = [functools.partial(shifted_copy, r) for r in range(1, 8)]
    vector_pieces += [functools.partial(conv, r0, c0) for r0 in range(0, ts, DW_ROWS)
                      for c0 in range(0, conv_width, 128)]

    st = {}

    def ffn_start():
        st["acc"] = xmid[...]
        st["h"] = _rmsnorm(st["acc"], fg_ref[...]).astype(BF16)

    def ffn_gate(c0):
        st["gate"] = jnp.dot(st["h"], wg_ref[:, c0:c0 + FF_CHUNK], preferred_element_type=F32)

    def ffn_up(c0):
        up = jnp.dot(st["h"], wu_ref[:, c0:c0 + FF_CHUNK], preferred_element_type=F32)
        st["a"] = (_silu(st["gate"]) * up).astype(BF16)

    def ffn_down(c0):
        st["acc"] = st["acc"] + jnp.dot(st["a"], wd_ref[c0:c0 + FF_CHUNK, :],
                                        preferred_element_type=F32)

    def ffn_finish():
        acc = st["acc"]
        if apply_final_norm:
            acc = _rmsnorm(acc, final_g_ref[...])
        o_ref[...] = acc

    matmul_pieces = [ffn_start]
    for c0 in range(0, d_ff, FF_CHUNK):
        matmul_pieces += [functools.partial(f, c0) for f in (ffn_gate, ffn_up, ffn_down)]
    matmul_pieces += [ffn_finish]

    @pl.when(s == 0)
    def _():
        for piece in vector_pieces:
            piece()

    @pl.when(s >= 1)
    def _():
        _interleave(matmul_pieces, vector_pieces)


def _odd_layer(x2, batch, seq, norm_g, w_in, pool_w, pool_scale, dw_w, dw_b, cn_g, cn_b, w_out,
               ffn_norm_g, j, w_gate, w_up, w_down, final_g, apply_final_norm):
    n_tok, d = x2.shape
    pool_width = pool_w.shape[0] * pool_w.shape[1]
    conv_width = dw_w.shape[1]
    d_ff = w_gate.shape[2]
    ts = SEQ_TILE
    nt = seq // ts
    n_tiles = batch * nt
    assert nt * ts == seq and ts % DW_ROWS == 0 and d_ff % FF_CHUNK == 0
    c2 = lambda s: (0, 0)
    c3 = lambda s: (0, 0, 0)
    pick = lambda s: (j, 0, 0)
    resident = functools.partial(pl.BlockSpec, pipeline_mode=pl.Buffered(1))
    return pl.pallas_call(
        functools.partial(_odd_layer_kernel, pool_width=pool_width, tiles_per_seq=nt,
                          n_tiles=n_tiles, apply_final_norm=apply_final_norm),
        grid=(n_tiles + 1,),
        in_specs=[
            pl.BlockSpec((ts, d), lambda s: (jnp.minimum(s, n_tiles - 1), 0)),
            pl.BlockSpec((1, d), c2),
            resident(w_in.shape, c2),
            pl.BlockSpec(pool_w.shape, c3),
            pl.BlockSpec((1, pool_width), c2),
            pl.BlockSpec(dw_w.shape, c2),
            pl.BlockSpec((1, conv_width), c2),
            pl.BlockSpec((1, conv_width), c2),
            pl.BlockSpec((1, conv_width), c2),
            resident(w_out.shape, c2),
            pl.BlockSpec((1, d), c2),
            resident((None, d, d_ff), pick),
            resident((None, d, d_ff), pick),
            resident((None, d_ff, d), pick),
            pl.BlockSpec((1, d), c2),
        ],
        out_specs=pl.BlockSpec((ts, d), lambda s: (jnp.maximum(s - 1, 0), 0)),
        out_shape=jax.ShapeDtypeStruct((n_tok, d), F32),
        scratch_shapes=[
            pltpu.VMEM((ts + POOL_HALO, pool_width), F32),
            pltpu.VMEM((ts + DW_HALO, conv_width), F32),
            pltpu.VMEM((7, ts + DW_HALO - 8, conv_width), F32),
            pltpu.VMEM((ts, conv_width), F32),
            pltpu.VMEM((ts, d), F32),
            pltpu.VMEM((ts, d), F32),
        ],
        compiler_params=pltpu.CompilerParams(
            dimension_semantics=("arbitrary",), vmem_limit_bytes=VMEM_LIMIT_BYTES),
        name="odd_layer",
    )(x2, norm_g.reshape(1, d), w_in.astype(BF16), pool_w.astype(BF16),
      pool_scale.reshape(1, pool_width), dw_w, dw_b.reshape(1, conv_width),
      cn_g.reshape(1, conv_width), cn_b.reshape(1, conv_width), w_out.astype(BF16),
      ffn_norm_g.reshape(1, d), w_gate, w_up, w_down, final_g.reshape(1, d))


PAIR_WIDTH = 2 * HEAD_DIM
N_PAIRS = N_HEADS // 2
PAIR_COLS = 4 * PAIR_WIDTH


def _chunk_cumsum_rows(x, n):
    idx = lax.broadcasted_iota(jnp.int32, x.shape, 0) % n
    s = 1
    while s < n:
        x = x + jnp.where(idx >= s, pltpu.roll(x, s, axis=0), 0.0)
        s *= 2
    return x


def _chunk_cumsum_lanes(x, n):
    idx = lax.broadcasted_iota(jnp.int32, x.shape, 1) % n
    s = 1
    while s < n:
        x = x + jnp.where(idx >= s, pltpu.roll(x, s, axis=1), 0.0)
        s *= 2
    return x


def _halves(a2):
    return a2[:, :HEAD_DIM], a2[:, HEAD_DIM:]


def _pair(a0, a1):
    return jnp.concatenate([a0, a1], axis=1)


def _block_diag(a2):
    a0, a1 = _halves(a2)
    z = jnp.zeros_like(a0)
    return jnp.concatenate([_pair(a0, z), _pair(z, a1)], axis=0)


def _pair_head_norm(x2, g2):
    x0, x1 = _halves(x2)
    return _pair(_normalize(x0), _normalize(x1)) * g2


def _rotary_pair(t2, cos, sin):
    t0, t1 = _halves(t2)
    return _pair(t0 * cos + pltpu.roll(t0, HEAD_DIM // 2, axis=1) * sin,
                 t1 * cos + pltpu.roll(t1, HEAD_DIM // 2, axis=1) * sin)


def _even_kernel(x_ref, g_ref, wm_ref, wr_ref, wgc_ref, wgr_ref, gbc_ref, gbr_ref, cw_ref, cb_ref,
                 cos_ref, sin_ref, mng_ref, rng_ref, wout_ref, o_ref,
                 zbuf, qkbuf, mixbuf, c_state, n_state, m_state, r_state, *, ret_log_decay):
    i = pl.program_id(1)
    ts = x_ref.shape[0]
    scale = HEAD_DIM ** -0.5
    n_chunks = ts // CHUNK

    @pl.when(i == 0)
    def _():
        zbuf[0:QK_HALO, :] = jnp.zeros((QK_HALO, zbuf.shape[1]), F32)
        c_state[...] = jnp.zeros(c_state.shape, F32)
        n_state[...] = jnp.zeros(n_state.shape, F32)
        m_state[...] = jnp.full(m_state.shape, NEG, F32)
        r_state[...] = jnp.zeros(r_state.shape, F32)

    x = x_ref[...]
    h = _rmsnorm(x, g_ref[...]).astype(BF16)
    width = N_HEADS * HEAD_DIM
    for pp in range(2 * N_PAIRS):
        w_ref = (wm_ref, wr_ref)[pp // N_PAIRS]
        for j in range(PAIR_COLS // PAIR_WIDTH):
            src = j * width + (pp % N_PAIRS) * PAIR_WIDTH
            dst = pp * PAIR_COLS + j * PAIR_WIDTH
            zbuf[QK_HALO:QK_HALO + ts, dst:dst + PAIR_WIDTH] = jnp.dot(
                h, w_ref[:, src:src + PAIR_WIDTH], preferred_element_type=F32)

    gate_c = jnp.dot(h, wgc_ref[...], preferred_element_type=F32) + gbc_ref[...]
    gate_r = _dot_nt(wgr_ref[...], h) + gbr_ref[...]
    ig_c = gate_c[:, 0:N_HEADS]
    b_c = _chunk_cumsum_rows(_log_sigmoid(gate_c), CHUNK)[:, N_HEADS:2 * N_HEADS]
    ig_r = gate_r[0:N_HEADS, :]
    b_r = _chunk_cumsum_lanes(_log_sigmoid(gate_r), CHUNK)[N_HEADS:2 * N_HEADS, :]

    row_i = lax.broadcasted_iota(jnp.int32, (CHUNK, CHUNK), 0)
    col_j = lax.broadcasted_iota(jnp.int32, (CHUNK, CHUNK), 1)
    causal = col_j <= row_i
    rel = (row_i - col_j).astype(F32)
    pos_c = lax.broadcasted_iota(jnp.int32, (CHUNK, 1), 0).astype(F32)
    blk_r = lax.broadcasted_iota(jnp.int32, (PAIR_WIDTH, PAIR_WIDTH), 0) // HEAD_DIM
    blk_c = lax.broadcasted_iota(jnp.int32, (PAIR_WIDTH, PAIR_WIDTH), 1) // HEAD_DIM
    same_head = blk_r == blk_c
    ones_row = jnp.ones((1, HEAD_DIM), F32)

    for p in range(N_PAIRS):
        base = p * PAIR_COLS
        qk_cols = slice(p * 2 * PAIR_WIDTH, (p + 1) * 2 * PAIR_WIDTH)
        q_cols = slice(p * PAIR_WIDTH, (p + 1) * PAIR_WIDTH)
        k_cols = slice(width + p * PAIR_WIDTH, width + (p + 1) * PAIR_WIDTH)
        conv = jnp.broadcast_to(_pair(cb_ref[:, q_cols], cb_ref[:, k_cols]), (ts, 2 * PAIR_WIDTH))
        for k in range(QK_CONV):
            start = QK_HALO - (QK_CONV - 1) + k
            w_k = _pair(cw_ref[k:k + 1, q_cols], cw_ref[k:k + 1, k_cols])
            conv = conv + w_k * zbuf[start:start + ts, base:base + 2 * PAIR_WIDTH]
        qkbuf[:, qk_cols] = _silu(conv)

    def mlstm_a(p, c, st):
        base = p * PAIR_COLS
        rows = slice(c * CHUNK, (c + 1) * CHUNK)
        zrows = slice(QK_HALO + c * CHUNK, QK_HALO + (c + 1) * CHUNK)
        q2 = qkbuf[rows, p * 2 * PAIR_WIDTH: p * 2 * PAIR_WIDTH + PAIR_WIDTH]
        k2 = qkbuf[rows, p * 2 * PAIR_WIDTH + PAIR_WIDTH: (p + 1) * 2 * PAIR_WIDTH] * scale
        v2 = zbuf[zrows, base + 2 * PAIR_WIDTH: base + 3 * PAIR_WIDTH]
        k_h = _halves(k2)
        log_d, m, inter, a_end, s_old, s_new, m_new = [], [], [], [], [], [], []
        for e in range(2):
            hd = 2 * p + e
            m_prev = st["m"][e]
            bc = b_c[rows, hd:hd + 1]
            ic = ig_c[rows, hd:hd + 1]
            br = b_r[hd:hd + 1, rows]
            ir = ig_r[hd:hd + 1, rows]
            ld = jnp.where(causal, bc - br + ir, NEG)
            m_intra = jnp.max(ld, axis=1, keepdims=True)
            total = bc[CHUNK - 1:CHUNK, :]
            w_end = total - bc + ic
            m_loc = jnp.max(w_end, axis=0, keepdims=True)
            log_inter = bc + m_prev
            m_e = jnp.maximum(m_intra, log_inter)
            mn = jnp.maximum(total + m_prev, m_loc)
            log_d.append(ld - m_e)
            m.append(m_e)
            inter.append(jnp.exp(log_inter - m_e))
            a_end.append(jnp.exp(w_end - m_loc))
            s_old.append(jnp.exp(total + m_prev - mn))
            s_new.append(jnp.exp(m_loc - mn))
            m_new.append(mn)
        ak2 = _pair(a_end[0] * k_h[0], a_end[1] * k_h[1])
        return dict(rows=rows, zrows=zrows, q2=q2, v2=v2, m=m, inter=inter,
                    decay=jnp.exp(_pair(log_d[0], log_d[1])),
                    qk=_dot_nt(q2, _block_diag(k2)),
                    q_state=_dot(q2, st["c"]),
                    kv=_dot_tn(ak2, v2),
                    n_sum=jnp.sum(ak2, axis=0, keepdims=True),
                    s_old2=_pair(s_old[0] * ones_row, s_old[1] * ones_row),
                    s_new2=_pair(s_new[0] * ones_row, s_new[1] * ones_row),
                    m_new=m_new)

    def mlstm_c(p, st, t):
        base = p * PAIR_COLS
        og2 = zbuf[t["zrows"], base + 3 * PAIR_WIDTH: base + 4 * PAIR_WIDTH]
        s_h = _halves(t["s2"])
        q_h = _halves(t["q2"])
        n_h = _halves(st["n"])
        hm = []
        for e in range(2):
            num = _halves(t["sv"])[e] + t["inter"][e] * _halves(t["q_state"])[e]
            den = (jnp.sum(s_h[e], axis=1, keepdims=True)
                   + t["inter"][e] * jnp.sum(q_h[e] * n_h[e], axis=1, keepdims=True))
            hm.append(num / jnp.maximum(jnp.abs(den), jnp.exp(-t["m"][e])))
        g2 = _pair(mng_ref[2 * p:2 * p + 1, :], mng_ref[2 * p + 1:2 * p + 2, :])
        hm2 = _pair_head_norm(_sigmoid(og2) * _pair(hm[0], hm[1]), g2)
        mixbuf[t["rows"], p * PAIR_WIDTH:(p + 1) * PAIR_WIDTH] = hm2.astype(BF16)
        st["c"] = t["s_old2"] * st["c"] + t["s_new2"] * jnp.where(same_head, t["kv"], 0.0)
        st["n"] = t["s_old2"] * st["n"] + t["s_new2"] * t["n_sum"]
        st["m"] = t["m_new"]

    def ret_a(p, c, st):
        base = (N_PAIRS + p) * PAIR_COLS
        rows = slice(c * CHUNK, (c + 1) * CHUNK)
        zrows = slice(QK_HALO + c * CHUNK, QK_HALO + (c + 1) * CHUNK)
        cos = cos_ref[rows, :]
        sin = sin_ref[rows, :]
        q2 = _rotary_pair(zbuf[zrows, base:base + PAIR_WIDTH], cos, sin)
        k2 = _rotary_pair(zbuf[zrows, base + PAIR_WIDTH:base + 2 * PAIR_WIDTH], cos, sin) * scale
        v2 = zbuf[zrows, base + 2 * PAIR_WIDTH: base + 3 * PAIR_WIDTH]
        return dict(rows=rows, zrows=zrows, v2=v2, decay=st["decay2"],
                    qk=_dot_nt(q2, _block_diag(k2)),
                    q_state=_dot(q2, st["r"]),
                    kv=_dot_tn(st["w_end2"] * k2, v2))

    def ret_c(p, st, t):
        base = (N_PAIRS + p) * PAIR_COLS
        rg2 = zbuf[t["zrows"], base + 3 * PAIR_WIDTH: base + 4 * PAIR_WIDTH]
        o2 = t["sv"] + t["q_state"] * st["q_decay2"]
        g2 = _pair(rng_ref[2 * p:2 * p + 1, :], rng_ref[2 * p + 1:2 * p + 2, :])
        hr2 = _silu(rg2) * _pair_head_norm(o2, g2)
        mixbuf[t["rows"], (N_PAIRS + p) * PAIR_WIDTH:(N_PAIRS + p + 1) * PAIR_WIDTH] = hr2.astype(BF16)
        st["r"] = st["chunk_decay2"] * st["r"] + jnp.where(same_head, t["kv"], 0.0)

    def stage_b(t):
        t["s2"] = t["qk"] * t["decay"]
        t["sv"] = _dot(t["s2"], _block_diag(t["v2"]))

    lanes = []
    for p in range(N_PAIRS):
        lanes.append((functools.partial(mlstm_a, p), functools.partial(mlstm_c, p),
                      dict(c=c_state[p], n=n_state[p],
                           m=[m_state[2 * p + e][:, 0:1] for e in range(2)])))
    for p in range(N_PAIRS):
        decay, w_end_r, q_decay, chunk_decay = [], [], [], []
        for e in range(2):
            lg = ret_log_decay[2 * p + e]
            decay.append(jnp.where(causal, jnp.exp(lg * jnp.maximum(rel, 0.0)), 0.0))
            w_end_r.append(jnp.exp(lg * (CHUNK - 1.0 - pos_c)) * ones_row)
            q_decay.append(jnp.exp(lg * (pos_c + 1.0)) * ones_row)
            chunk_decay.append(math.exp(lg * CHUNK) * ones_row)
        lanes.append((functools.partial(ret_a, p), functools.partial(ret_c, p),
                      dict(r=r_state[p], decay2=_pair(*decay), w_end2=_pair(*w_end_r),
                           q_decay2=_pair(*q_decay), chunk_decay2=_pair(*chunk_decay))))

    for c in range(n_chunks):
        work = [stage_a(c, st) for stage_a, _, st in lanes]
        for t in work:
            stage_b(t)
        for (_, stage_c, st), t in zip(lanes, work):
            stage_c(st, t)

    for p in range(N_PAIRS):
        st = lanes[p][2]
        c_state[p] = st["c"]
        n_state[p] = st["n"]
        for e in range(2):
            m_state[2 * p + e] = st["m"][e] * ones_row
        r_state[p] = lanes[N_PAIRS + p][2]["r"]

    o_ref[...] = x + jnp.dot(mixbuf[...], wout_ref[...], preferred_element_type=F32)
    zbuf[0:QK_HALO, :] = zbuf[ts:ts + QK_HALO, :]


def _retention_log_decays():
    h = np.arange(N_HEADS, dtype=np.float32)
    lg = np.log(np.float32(1.0) - np.float32(2.0) ** (np.float32(-5.0) - h)).astype(np.float32)
    return tuple(float(v) for v in lg)


@functools.lru_cache(maxsize=None)
def _rotary_tables(seq):
    inv = np.float32(ROPE_BASE) ** (-np.arange(0, HEAD_DIM, 2, dtype=np.float32) / np.float32(HEAD_DIM))
    ang = np.arange(seq, dtype=np.float32)[:, None] * inv[None, :]
    cos = np.cos(ang).astype(np.float32)
    sin = np.sin(ang).astype(np.float32)
    return np.concatenate([cos, cos], axis=-1), np.concatenate([-sin, sin], axis=-1)


def _even_mixer(x2, batch, seq, norm_g, w_in, qk_conv_w, qk_conv_b, i_bias, f_bias,
                mlstm_norm_g, ret_norm_g, w_out):
    n_tok, d = x2.shape
    width = N_HEADS * HEAD_DIM
    ts = SEQ_TILE
    nt = seq // ts
    assert nt * ts == seq and ts % CHUNK == 0
    g0 = 4 * width
    g1 = g0 + 2 * N_HEADS
    w_m = w_in[:, :g0].astype(BF16)
    w_r = w_in[:, g1:].astype(BF16)
    w_gate = w_in[:, g0:g1].astype(BF16)
    w_gate_c = jnp.pad(w_gate, ((0, 0), (0, HEAD_DIM - 2 * N_HEADS)))
    w_gate_r = w_gate.T
    gate_b = jnp.concatenate([i_bias, f_bias]).astype(F32)
    gate_b_c = jnp.pad(gate_b, (0, HEAD_DIM - 2 * N_HEADS)).reshape(1, HEAD_DIM)
    gate_b_r = gate_b.reshape(2 * N_HEADS, 1)
    cos_t, sin_t = _rotary_tables(seq)

    row = lambda b, i: (b * nt + i, 0)
    seq_row = lambda b, i: (i, 0)
    c2 = lambda b, i: (0, 0)
    return pl.pallas_call(
        functools.partial(_even_kernel, ret_log_decay=_retention_log_decays()),
        grid=(batch, nt),
        in_specs=[
            pl.BlockSpec((ts, d), row),
            pl.BlockSpec((1, d), c2),
            pl.BlockSpec(w_m.shape, c2),
            pl.BlockSpec(w_r.shape, c2),
            pl.BlockSpec(w_gate_c.shape, c2),
            pl.BlockSpec(w_gate_r.shape, c2),
            pl.BlockSpec(gate_b_c.shape, c2),
            pl.BlockSpec(gate_b_r.shape, c2),
            pl.BlockSpec(qk_conv_w.shape, c2),
            pl.BlockSpec((1, 2 * width), c2),
            pl.BlockSpec((ts, HEAD_DIM), seq_row),
            pl.BlockSpec((ts, HEAD_DIM), seq_row),
            pl.BlockSpec(mlstm_norm_g.shape, c2),
            pl.BlockSpec(ret_norm_g.shape, c2),
            pl.BlockSpec(w_out.shape, c2),
        ],
        out_specs=pl.BlockSpec((ts, d), row),
        out_shape=jax.ShapeDtypeStruct((n_tok, d), F32),
        scratch_shapes=[
            pltpu.VMEM((ts + QK_HALO, 8 * width), F32),
            pltpu.VMEM((ts, 2 * width), F32),
            pltpu.VMEM((ts, 2 * width), BF16),
            pltpu.VMEM((N_PAIRS, PAIR_WIDTH, PAIR_WIDTH), F32),
            pltpu.VMEM((N_PAIRS, 1, PAIR_WIDTH), F32),
            pltpu.VMEM((N_HEADS, 1, HEAD_DIM), F32),
            pltpu.VMEM((N_PAIRS, PAIR_WIDTH, PAIR_WIDTH), F32),
        ],
        compiler_params=pltpu.CompilerParams(
            dimension_semantics=("arbitrary", "arbitrary"), vmem_limit_bytes=VMEM_LIMIT_BYTES),
        name="even_mixer",
    )(x2, norm_g.reshape(1, d), w_m, w_r, w_gate_c, w_gate_r, gate_b_c, gate_b_r,
      qk_conv_w, qk_conv_b.reshape(1, 2 * width), jnp.asarray(cos_t), jnp.asarray(sin_t),
      mlstm_norm_g, ret_norm_g, w_out.astype(BF16))


def kernel(x, norm_mix_g, norm_ffn_g, final_norm_g, ev_w_in, ev_qk_conv_w, ev_qk_conv_b, ev_i_bias, ev_f_bias, ev_mlstm_norm_g, ev_ret_norm_g, ev_w_out, od_w_in, od_pool_w, od_pool_scale, od_dw_w, od_dw_b, od_conv_norm_g, od_conv_norm_b, od_w_out, ffn_w_gate, ffn_w_up, ffn_w_down):
    batch, seq, d = x.shape
    depth = norm_mix_g.shape[0]
    x2 = x.reshape(batch * seq, d)
    odd_w_gate, odd_w_up, odd_w_down = (w[1::2].astype(BF16)
                                        for w in (ffn_w_gate, ffn_w_up, ffn_w_down))
    for layer in range(depth):
        j = layer // 2
        if layer % 2 == 0:
            x2 = _even_mixer(x2, batch, seq, norm_mix_g[layer], ev_w_in[j], ev_qk_conv_w[j],
                             ev_qk_conv_b[j], ev_i_bias[j], ev_f_bias[j], ev_mlstm_norm_g[j],
                             ev_ret_norm_g[j], ev_w_out[j])
            x2 = _ffn(x2, layer, norm_ffn_g[layer], ffn_w_gate, ffn_w_up, ffn_w_down,
                      final_norm_g, apply_final_norm=(layer == depth - 1))
        else:
            x2 = _odd_layer(x2, batch, seq, norm_mix_g[layer], od_w_in[j], od_pool_w[j],
                            od_pool_scale[j], od_dw_w[j], od_dw_b[j], od_conv_norm_g[j],
                            od_conv_norm_b[j], od_w_out[j], norm_ffn_g[layer], j, odd_w_gate,
                            odd_w_up, odd_w_down, final_norm_g,
                            apply_final_norm=(layer == depth - 1))
    return x2.reshape(batch, seq, d)
```

```python
import functools
import math

import numpy as np
import jax
import jax.numpy as jnp
from jax import lax
from jax.experimental import pallas as pl
from jax.experimental.pallas import tpu as pltpu

F32 = jnp.float32
BF16 = jnp.bfloat16

EPS = 1e-6
NEG = -1e30
CHUNK = 128
HEAD_DIM = 128
N_HEADS = 4
QK_CONV = 4
ROPE_BASE = 10000.0
POOL_WINDOWS = (2, 4, 8, 16)
POOL_GROUP = 128
DW_KERNEL = 31

SEQ_TILE = 512
FFN_TILE = 512
FF_CHUNK = 256
QK_HALO = 8
POOL_HALO = 16
DW_HALO = 32
DW_ROWS = 64
VMEM_LIMIT_BYTES = 56 * 1024 * 1024


def _rmsnorm(x, g):
    ms = jnp.mean(x * x, axis=-1, keepdims=True)
    return (x * lax.rsqrt(ms + EPS)) * g


def _normalize(x):
    xc = x - jnp.mean(x, axis=-1, keepdims=True)
    return xc * lax.rsqrt(jnp.mean(xc * xc, axis=-1, keepdims=True) + EPS)


def _sigmoid(x):
    return 1.0 / (1.0 + jnp.exp(-x))


def _silu(x):
    return x * _sigmoid(x)


def _log_sigmoid(x):
    return jnp.minimum(x, 0.0) - jnp.log(1.0 + jnp.exp(-jnp.abs(x)))


def _dot(a, b):
    return jnp.dot(a.astype(BF16), b.astype(BF16), preferred_element_type=F32)


def _dot_nt(a, b):
    return lax.dot_general(a.astype(BF16), b.astype(BF16), (((1,), (1,)), ((), ())),
                           preferred_element_type=F32)


def _dot_tn(a, b):
    return lax.dot_general(a.astype(BF16), b.astype(BF16), (((0,), (0,)), ((), ())),
                           preferred_element_type=F32)


def _ffn_kernel(x_ref, g_ref, wg_ref, wu_ref, wd_ref, fg_ref, o_ref, *, apply_final_norm):
    x = x_ref[...]
    h = _rmsnorm(x, g_ref[...]).astype(BF16)
    acc = x
    d_ff = wg_ref.shape[1]
    for c0 in range(0, d_ff, FF_CHUNK):
        cols = slice(c0, c0 + FF_CHUNK)
        gate = jnp.dot(h, wg_ref[:, cols], preferred_element_type=F32)
        up = jnp.dot(h, wu_ref[:, cols], preferred_element_type=F32)
        a = (_silu(gate) * up).astype(BF16)
        acc = acc + jnp.dot(a, wd_ref[cols, :], preferred_element_type=F32)
    if apply_final_norm:
        acc = _rmsnorm(acc, fg_ref[...])
    o_ref[...] = acc


def _ffn(x2, layer, norm_g, w_gate, w_up, w_down, final_g, apply_final_norm):
    n_tok, d = x2.shape
    d_ff = w_gate.shape[2]
    assert d_ff % FF_CHUNK == 0 and n_tok % FFN_TILE == 0
    const2 = lambda i: (0, 0)
    pick = lambda i: (layer, 0, 0)
    return pl.pallas_call(
        functools.partial(_ffn_kernel, apply_final_norm=apply_final_norm),
        grid=(n_tok // FFN_TILE,),
        in_specs=[
            pl.BlockSpec((FFN_TILE, d), lambda i: (i, 0)),
            pl.BlockSpec((1, d), const2),
            pl.BlockSpec((None, d, d_ff), pick, pipeline_mode=pl.Buffered(1)),
            pl.BlockSpec((None, d, d_ff), pick, pipeline_mode=pl.Buffered(1)),
            pl.BlockSpec((None, d_ff, d), pick, pipeline_mode=pl.Buffered(1)),
            pl.BlockSpec((1, d), const2),
        ],
        out_specs=pl.BlockSpec((FFN_TILE, d), lambda i: (i, 0)),
        out_shape=jax.ShapeDtypeStruct((n_tok, d), F32),
        compiler_params=pltpu.CompilerParams(
            dimension_semantics=("arbitrary",), vmem_limit_bytes=VMEM_LIMIT_BYTES),
        name="ffn",
    )(x2, norm_g.reshape(1, d), w_gate, w_up, w_down, final_g.reshape(1, d))


def _interleave(main, fill):
    n = len(main)
    for s, fn in enumerate(main):
        fn()
        for g in fill[s * len(fill) // n:(s + 1) * len(fill) // n]:
            g()


def _odd_layer_kernel(x_ref, g_ref, win_ref, poolw_ref, pscale_ref, dww_ref, dwb_ref, cng_ref,
                      cnb_ref, wout_ref, fg_ref, wg_ref, wu_ref, wd_ref, final_g_ref, o_ref,
                      ubuf, ybuf, yrot, cbuf, pool_out, xmid,
                      *, pool_width, tiles_per_seq, n_tiles, apply_final_norm):
    s = pl.program_id(0)
    ts = x_ref.shape[0]
    conv_width = ybuf.shape[1]
    d_ff = wg_ref.shape[1]
    seq_tile = s % tiles_per_seq

    @pl.when(seq_tile == 0)
    def _():
        ubuf[0:POOL_HALO, :] = jnp.zeros((POOL_HALO, ubuf.shape[1]), F32)
        ybuf[0:DW_HALO, :] = jnp.zeros((DW_HALO, ybuf.shape[1]), F32)

    @pl.when(seq_tile != 0)
    def _():
        ubuf[0:POOL_HALO, :] = ubuf[ts:ts + POOL_HALO, :]
        ybuf[0:DW_HALO, :] = ybuf[ts:ts + DW_HALO, :]

    @pl.when(s >= 1)
    def _():
        yf = _normalize(cbuf[...]) * cng_ref[...] + cnb_ref[...]
        xmid[...] = pool_out[...] + _dot(_silu(yf), wout_ref[pool_width:, :])

    pool_st = {}

    def pool(g, w):
        cols = slice(g * POOL_GROUP, (g + 1) * POOL_GROUP)
        pos = (seq_tile * ts + lax.broadcasted_iota(jnp.int32, (ts, 1), 0)).astype(F32) + 1.0
        wsum = ubuf[:, cols]
        shift = 1
        while shift < w:
            wsum = wsum + pltpu.roll(wsum, shift, axis=0)
            shift *= 2
        cur = ubuf[POOL_HALO:POOL_HALO + ts, cols]
        pooled = wsum[POOL_HALO:, :] / jnp.minimum(pos, float(w)) - cur
        y_pool = _dot(pooled, poolw_ref[g]) * pscale_ref[:, cols]
        prev = x_ref[...] if g == 0 else pool_st["out"]
        pool_st["out"] = prev + _dot(y_pool, wout_ref[cols, :])
        if g == len(POOL_WINDOWS) - 1:
            pool_out[...] = pool_st["out"]

    @pl.when(s < n_tiles)
    def _():
        h = _rmsnorm(x_ref[...], g_ref[...]).astype(BF16)
        z = jnp.dot(h, win_ref[...], preferred_element_type=F32)
        ubuf[POOL_HALO:POOL_HALO + ts, :] = z[:, :pool_width]
        ga = z[:, pool_width:pool_width + conv_width]
        gb = z[:, pool_width + conv_width:]
        ybuf[DW_HALO:DW_HALO + ts, :] = ga * _sigmoid(gb)
        for g, w in enumerate(POOL_WINDOWS):
            pool(g, w)

    rot_rows = yrot.shape[1]

    def shifted_copy(r):
        yrot[r - 1] = ybuf[r:r + rot_rows, :]

    def conv(r0, c0):
        cols = slice(c0, c0 + 128)
        acc = jnp.broadcast_to(dwb_ref[:, cols], (DW_ROWS, 128))
        for k in range(DW_KERNEL):
            start = DW_HALO - (DW_KERNEL - 1) + k + r0
            src = ybuf if start % 8 == 0 else yrot.at[start % 8 - 1]
            aligned = start - start % 8
            acc = acc + dww_ref[k:k + 1, cols] * src[aligned:aligned + DW_ROWS, cols]
        cbuf[r0:r0 + DW_ROWS, cols] = acc

    vector_pieces = [functools.partial(shifted_copy, r) for r in range(1, 8)]
    vector_pieces += [functools.partial(conv, r0, c0) for r0 in range(0, ts, DW_ROWS)
                      for c0 in range(0, conv_width, 128)]

    st = {}

    def ffn_start():
        st["acc"] = xmid[...]
        st["h"] = _rmsnorm(st["acc"], fg_ref[...]).astype(BF16)

    def ffn_gate(c0):
        st["gate"] = jnp.dot(st["h"], wg_ref[:, c0:c0 + FF_CHUNK], preferred_element_type=F32)

    def ffn_up(c0):
        up = jnp.dot(st["h"], wu_ref[:, c0:c0 + FF_CHUNK], preferred_element_type=F32)
        st["a"] = (_silu(st["gate"]) * up).astype(BF16)

    def ffn_down(c0):
        st["acc"] = st["acc"] + jnp.dot(st["a"], wd_ref[c0:c0 + FF_CHUNK, :],
                                        preferred_element_type=F32)

    def ffn_finish():
        acc = st["acc"]
        if apply_final_norm:
            acc = _rmsnorm(acc, final_g_ref[...])
        o_ref[...] = acc

    matmul_pieces = [ffn_start]
    for c0 in range(0, d_ff, FF_CHUNK):
        matmul_pieces += [functools.partial(f, c0) for f in (ffn_gate, ffn_up, ffn_down)]
    matmul_pieces += [ffn_finish]

    @pl.when(s == 0)
    def _():
        for piece in vector_pieces:
            piece()

    @pl.when(s >= 1)
    def _():
        _interleave(matmul_pieces, vector_pieces)


def _odd_layer(x2, batch, seq, norm_g, w_in, pool_w, pool_scale, dw_w, dw_b, cn_g, cn_b, w_out,
               ffn_norm_g, layer, w_gate, w_up, w_down, final_g, apply_final_norm):
    n_tok, d = x2.shape
    pool_width = pool_w.shape[0] * pool_w.shape[1]
    conv_width = dw_w.shape[1]
    d_ff = w_gate.shape[2]
    ts = SEQ_TILE
    nt = seq // ts
    n_tiles = batch * nt
    assert nt * ts == seq and ts % DW_ROWS == 0 and d_ff % FF_CHUNK == 0
    c2 = lambda s: (0, 0)
    c3 = lambda s: (0, 0, 0)
    pick = lambda s: (layer, 0, 0)
    resident = functools.partial(pl.BlockSpec, pipeline_mode=pl.Buffered(1))
    return pl.pallas_call(
        functools.partial(_odd_layer_kernel, pool_width=pool_width, tiles_per_seq=nt,
                          n_tiles=n_tiles, apply_final_norm=apply_final_norm),
        grid=(n_tiles + 1,),
        in_specs=[
            pl.BlockSpec((ts, d), lambda s: (jnp.minimum(s, n_tiles - 1), 0)),
            pl.BlockSpec((1, d), c2),
            resident(w_in.shape, c2),
            pl.BlockSpec(pool_w.shape, c3),
            pl.BlockSpec((1, pool_width), c2),
            pl.BlockSpec(dw_w.shape, c2),
            pl.BlockSpec((1, conv_width), c2),
            pl.BlockSpec((1, conv_width), c2),
            pl.BlockSpec((1, conv_width), c2),
            resident(w_out.shape, c2),
            pl.BlockSpec((1, d), c2),
            resident((None, d, d_ff), pick),
            resident((None, d, d_ff), pick),
            resident((None, d_ff, d), pick),
            pl.BlockSpec((1, d), c2),
        ],
        out_specs=pl.BlockSpec((ts, d), lambda s: (jnp.maximum(s - 1, 0), 0)),
        out_shape=jax.ShapeDtypeStruct((n_tok, d), F32),
        scratch_shapes=[
            pltpu.VMEM((ts + POOL_HALO, pool_width), F32),
            pltpu.VMEM((ts + DW_HALO, conv_width), F32),
            pltpu.VMEM((7, ts + DW_HALO - 8, conv_width), F32),
            pltpu.VMEM((ts, conv_width), F32),
            pltpu.VMEM((ts, d), F32),
            pltpu.VMEM((ts, d), F32),
        ],
        compiler_params=pltpu.CompilerParams(
            dimension_semantics=("arbitrary",), vmem_limit_bytes=VMEM_LIMIT_BYTES),
        name="odd_layer",
    )(x2, norm_g.reshape(1, d), w_in.astype(BF16), pool_w.astype(BF16),
      pool_scale.reshape(1, pool_width), dw_w, dw_b.reshape(1, conv_width),
      cn_g.reshape(1, conv_width), cn_b.reshape(1, conv_width), w_out.astype(BF16),
      ffn_norm_g.reshape(1, d), w_gate, w_up, w_down, final_g.reshape(1, d))


PAIR_WIDTH = 2 * HEAD_DIM
N_PAIRS = N_HEADS // 2
PAIR_COLS = 4 * PAIR_WIDTH


def _chunk_cumsum_rows(x, n):
    idx = lax.broadcasted_iota(jnp.int32, x.shape, 0) % n
    s = 1
    while s < n:
        x = x + jnp.where(idx >= s, pltpu.roll(x, s, axis=0), 0.0)
        s *= 2
    return x


def _chunk_cumsum_lanes(x, n):
    idx = lax.broadcasted_iota(jnp.int32, x.shape, 1) % n
    s = 1
    while s < n:
        x = x + jnp.where(idx >= s, pltpu.roll(x, s, axis=1), 0.0)
        s *= 2
    return x


def _halves(a2):
    return a2[:, :HEAD_DIM], a2[:, HEAD_DIM:]


def _pair(a0, a1):
    return jnp.concatenate([a0, a1], axis=1)


def _block_diag(a2):
    a0, a1 = _halves(a2)
    z = jnp.zeros_like(a0)
    return jnp.concatenate([_pair(a0, z), _pair(z, a1)], axis=0)


def _pair_head_norm(x2, g2):
    x0, x1 = _halves(x2)
    return _pair(_normalize(x0), _normalize(x1)) * g2


def _rotary_pair(t2, cos, sin):
    t0, t1 = _halves(t2)
    return _pair(t0 * cos + pltpu.roll(t0, HEAD_DIM // 2, axis=1) * sin,
                 t1 * cos + pltpu.roll(t1, HEAD_DIM // 2, axis=1) * sin)


def _even_kernel(x_ref, g_ref, wm_ref, wr_ref, wgc_ref, wgr_ref, gbc_ref, gbr_ref, cw_ref, cb_ref,
                 cos_ref, sin_ref, mng_ref, rng_ref, wout_ref, o_ref,
                 zbuf, qkbuf, mixbuf, c_state, n_state, m_state, r_state, *, ret_log_decay):
    i = pl.program_id(1)
    ts = x_ref.shape[0]
    scale = HEAD_DIM ** -0.5
    n_chunks = ts // CHUNK

    @pl.when(i == 0)
    def _():
        zbuf[0:QK_HALO, :] = jnp.zeros((QK_HALO, zbuf.shape[1]), F32)
        c_state[...] = jnp.zeros(c_state.shape, F32)
        n_state[...] = jnp.zeros(n_state.shape, F32)
        m_state[...] = jnp.full(m_state.shape, NEG, F32)
        r_state[...] = jnp.zeros(r_state.shape, F32)

    x = x_ref[...]
    h = _rmsnorm(x, g_ref[...]).astype(BF16)
    width = N_HEADS * HEAD_DIM
    for pp in range(2 * N_PAIRS):
        w_ref = (wm_ref, wr_ref)[pp // N_PAIRS]
        for j in range(PAIR_COLS // PAIR_WIDTH):
            src = j * width + (pp % N_PAIRS) * PAIR_WIDTH
            dst = pp * PAIR_COLS + j * PAIR_WIDTH
            zbuf[QK_HALO:QK_HALO + ts, dst:dst + PAIR_WIDTH] = jnp.dot(
                h, w_ref[:, src:src + PAIR_WIDTH], preferred_element_type=F32)

    gate_c = jnp.dot(h, wgc_ref[...], preferred_element_type=F32) + gbc_ref[...]
    gate_r = _dot_nt(wgr_ref[...], h) + gbr_ref[...]
    ig_c = gate_c[:, 0:N_HEADS]
    b_c = _chunk_cumsum_rows(_log_sigmoid(gate_c), CHUNK)[:, N_HEADS:2 * N_HEADS]
    ig_r = gate_r[0:N_HEADS, :]
    b_r = _chunk_cumsum_lanes(_log_sigmoid(gate_r), CHUNK)[N_HEADS:2 * N_HEADS, :]

    row_i = lax.broadcasted_iota(jnp.int32, (CHUNK, CHUNK), 0)
    col_j = lax.broadcasted_iota(jnp.int32, (CHUNK, CHUNK), 1)
    causal = col_j <= row_i
    rel = (row_i - col_j).astype(F32)
    pos_c = lax.broadcasted_iota(jnp.int32, (CHUNK, 1), 0).astype(F32)
    blk_r = lax.broadcasted_iota(jnp.int32, (PAIR_WIDTH, PAIR_WIDTH), 0) // HEAD_DIM
    blk_c = lax.broadcasted_iota(jnp.int32, (PAIR_WIDTH, PAIR_WIDTH), 1) // HEAD_DIM
    same_head = blk_r == blk_c
    ones_row = jnp.ones((1, HEAD_DIM), F32)

    for p in range(N_PAIRS):
        base = p * PAIR_COLS
        qk_cols = slice(p * 2 * PAIR_WIDTH, (p + 1) * 2 * PAIR_WIDTH)
        q_cols = slice(p * PAIR_WIDTH, (p + 1) * PAIR_WIDTH)
        k_cols = slice(width + p * PAIR_WIDTH, width + (p + 1) * PAIR_WIDTH)
        conv = jnp.broadcast_to(_pair(cb_ref[:, q_cols], cb_ref[:, k_cols]), (ts, 2 * PAIR_WIDTH))
        for k in range(QK_CONV):
            start = QK_HALO - (QK_CONV - 1) + k
            w_k = _pair(cw_ref[k:k + 1, q_cols], cw_ref[k:k + 1, k_cols])
            conv = conv + w_k * zbuf[start:start + ts, base:base + 2 * PAIR_WIDTH]
        qkbuf[:, qk_cols] = _silu(conv)

    def mlstm_a(p, c, st):
        base = p * PAIR_COLS
        rows = slice(c * CHUNK, (c + 1) * CHUNK)
        zrows = slice(QK_HALO + c * CHUNK, QK_HALO + (c + 1) * CHUNK)
        q2 = qkbuf[rows, p * 2 * PAIR_WIDTH: p * 2 * PAIR_WIDTH + PAIR_WIDTH]
        k2 = qkbuf[rows, p * 2 * PAIR_WIDTH + PAIR_WIDTH: (p + 1) * 2 * PAIR_WIDTH] * scale
        v2 = zbuf[zrows, base + 2 * PAIR_WIDTH: base + 3 * PAIR_WIDTH]
        k_h = _halves(k2)
        log_d, m, inter, a_end, s_old, s_new, m_new = [], [], [], [], [], [], []
        for e in range(2):
            hd = 2 * p + e
            m_prev = st["m"][e]
            bc = b_c[rows, hd:hd + 1]
            ic = ig_c[rows, hd:hd + 1]
            br = b_r[hd:hd + 1, rows]
            ir = ig_r[hd:hd + 1, rows]
            ld = jnp.where(causal, bc - br + ir, NEG)
            m_intra = jnp.max(ld, axis=1, keepdims=True)
            total = bc[CHUNK - 1:CHUNK, :]
            w_end = total - bc + ic
            m_loc = jnp.max(w_end, axis=0, keepdims=True)
            log_inter = bc + m_prev
            m_e = jnp.maximum(m_intra, log_inter)
            mn = jnp.maximum(total + m_prev, m_loc)
            log_d.append(ld - m_e)
            m.append(m_e)
            inter.append(jnp.exp(log_inter - m_e))
            a_end.append(jnp.exp(w_end - m_loc))
            s_old.append(jnp.exp(total + m_prev - mn))
            s_new.append(jnp.exp(m_loc - mn))
            m_new.append(mn)
        ak2 = _pair(a_end[0] * k_h[0], a_end[1] * k_h[1])
        return dict(rows=rows, zrows=zrows, q2=q2, v2=v2, m=m, inter=inter,
                    decay=jnp.exp(_pair(log_d[0], log_d[1])),
                    qk=_dot_nt(q2, _block_diag(k2)),
                    q_state=_dot(q2, st["c"]),
                    kv=_dot_tn(ak2, v2),
                    n_sum=jnp.sum(ak2, axis=0, keepdims=True),
                    s_old2=_pair(s_old[0] * ones_row, s_old[1] * ones_row),
                    s_new2=_pair(s_new[0] * ones_row, s_new[1] * ones_row),
                    m_new=m_new)

    def mlstm_c(p, st, t):
        base = p * PAIR_COLS
        og2 = zbuf[t["zrows"], base + 3 * PAIR_WIDTH: base + 4 * PAIR_WIDTH]
        s_h = _halves(t["s2"])
        q_h = _halves(t["q2"])
        n_h = _halves(st["n"])
        hm = []
        for e in range(2):
            num = _halves(t["sv"])[e] + t["inter"][e] * _halves(t["q_state"])[e]
            den = (jnp.sum(s_h[e], axis=1, keepdims=True)
                   + t["inter"][e] * jnp.sum(q_h[e] * n_h[e], axis=1, keepdims=True))
            hm.append(num / jnp.maximum(jnp.abs(den), jnp.exp(-t["m"][e])))
        g2 = _pair(mng_ref[2 * p:2 * p + 1, :], mng_ref[2 * p + 1:2 * p + 2, :])
        hm2 = _pair_head_norm(_sigmoid(og2) * _pair(hm[0], hm[1]), g2)
        mixbuf[t["rows"], p * PAIR_WIDTH:(p + 1) * PAIR_WIDTH] = hm2.astype(BF16)
        st["c"] = t["s_old2"] * st["c"] + t["s_new2"] * jnp.where(same_head, t["kv"], 0.0)
        st["n"] = t["s_old2"] * st["n"] + t["s_new2"] * t["n_sum"]
        st["m"] = t["m_new"]

    def ret_a(p, c, st):
        base = (N_PAIRS + p) * PAIR_COLS
        rows = slice(c * CHUNK, (c + 1) * CHUNK)
        zrows = slice(QK_HALO + c * CHUNK, QK_HALO + (c + 1) * CHUNK)
        cos = cos_ref[rows, :]
        sin = sin_ref[rows, :]
        q2 = _rotary_pair(zbuf[zrows, base:base + PAIR_WIDTH], cos, sin)
        k2 = _rotary_pair(zbuf[zrows, base + PAIR_WIDTH:base + 2 * PAIR_WIDTH], cos, sin) * scale
        v2 = zbuf[zrows, base + 2 * PAIR_WIDTH: base + 3 * PAIR_WIDTH]
        return dict(rows=rows, zrows=zrows, v2=v2, decay=st["decay2"],
                    qk=_dot_nt(q2, _block_diag(k2)),
                    q_state=_dot(q2, st["r"]),
                    kv=_dot_tn(st["w_end2"] * k2, v2))

    def ret_c(p, st, t):
        base = (N_PAIRS + p) * PAIR_COLS
        rg2 = zbuf[t["zrows"], base + 3 * PAIR_WIDTH: base + 4 * PAIR_WIDTH]
        o2 = t["sv"] + t["q_state"] * st["q_decay2"]
        g2 = _pair(rng_ref[2 * p:2 * p + 1, :], rng_ref[2 * p + 1:2 * p + 2, :])
        hr2 = _silu(rg2) * _pair_head_norm(o2, g2)
        mixbuf[t["rows"], (N_PAIRS + p) * PAIR_WIDTH:(N_PAIRS + p + 1) * PAIR_WIDTH] = hr2.astype(BF16)
        st["r"] = st["chunk_decay2"] * st["r"] + jnp.where(same_head, t["kv"], 0.0)

    def stage_b(t):
        t["s2"] = t["qk"] * t["decay"]
        t["sv"] = _dot(t["s2"], _block_diag(t["v2"]))

    lanes = []
    for p in range(N_PAIRS):
        lanes.append((functools.partial(mlstm_a, p), functools.partial(mlstm_c, p),
                      dict(c=c_state[p], n=n_state[p],
                           m=[m_state[2 * p + e][:, 0:1] for e in range(2)])))
    for p in range(N_PAIRS):
        decay, w_end_r, q_decay, chunk_decay = [], [], [], []
        for e in range(2):
            lg = ret_log_decay[2 * p + e]
            decay.append(jnp.where(causal, jnp.exp(lg * jnp.maximum(rel, 0.0)), 0.0))
            w_end_r.append(jnp.exp(lg * (CHUNK - 1.0 - pos_c)) * ones_row)
            q_decay.append(jnp.exp(lg * (pos_c + 1.0)) * ones_row)
            chunk_decay.append(math.exp(lg * CHUNK) * ones_row)
        lanes.append((functools.partial(ret_a, p), functools.partial(ret_c, p),
                      dict(r=r_state[p], decay2=_pair(*decay), w_end2=_pair(*w_end_r),
                           q_decay2=_pair(*q_decay), chunk_decay2=_pair(*chunk_decay))))

    for c in range(n_chunks):
        work = [stage_a(c, st) for stage_a, _, st in lanes]
        for t in work:
            stage_b(t)
        for (_, stage_c, st), t in zip(lanes, work):
            stage_c(st, t)

    for p in range(N_PAIRS):
        st = lanes[p][2]
        c_state[p] = st["c"]
        n_state[p] = st["n"]
        for e in range(2):
            m_state[2 * p + e] = st["m"][e] * ones_row
        r_state[p] = lanes[N_PAIRS + p][2]["r"]

    o_ref[...] = x + jnp.dot(mixbuf[...], wout_ref[...], preferred_element_type=F32)
    zbuf[0:QK_HALO, :] = zbuf[ts:ts + QK_HALO, :]


def _retention_log_decays():
    h = np.arange(N_HEADS, dtype=np.float32)
    lg = np.log(np.float32(1.0) - np.float32(2.0) ** (np.float32(-5.0) - h)).astype(np.float32)
    return tuple(float(v) for v in lg)


@functools.lru_cache(maxsize=None)
def _rotary_tables(seq):
    inv = np.float32(ROPE_BASE) ** (-np.arange(0, HEAD_DIM, 2, dtype=np.float32) / np.float32(HEAD_DIM))
    ang = np.arange(seq, dtype=np.float32)[:, None] * inv[None, :]
    cos = np.cos(ang).astype(np.float32)
    sin = np.sin(ang).astype(np.float32)
    return np.concatenate([cos, cos], axis=-1), np.concatenate([-sin, sin], axis=-1)


def _even_mixer(x2, batch, seq, norm_g, w_in, qk_conv_w, qk_conv_b, i_bias, f_bias,
                mlstm_norm_g, ret_norm_g, w_out):
    n_tok, d = x2.shape
    width = N_HEADS * HEAD_DIM
    ts = SEQ_TILE
    nt = seq // ts
    assert nt * ts == seq and ts % CHUNK == 0
    g0 = 4 * width
    g1 = g0 + 2 * N_HEADS
    w_m = w_in[:, :g0].astype(BF16)
    w_r = w_in[:, g1:].astype(BF16)
    w_gate = w_in[:, g0:g1].astype(BF16)
    w_gate_c = jnp.pad(w_gate, ((0, 0), (0, HEAD_DIM - 2 * N_HEADS)))
    w_gate_r = w_gate.T
    gate_b = jnp.concatenate([i_bias, f_bias]).astype(F32)
    gate_b_c = jnp.pad(gate_b, (0, HEAD_DIM - 2 * N_HEADS)).reshape(1, HEAD_DIM)
    gate_b_r = gate_b.reshape(2 * N_HEADS, 1)
    cos_t, sin_t = _rotary_tables(seq)

    row = lambda b, i: (b * nt + i, 0)
    seq_row = lambda b, i: (i, 0)
    c2 = lambda b, i: (0, 0)
    return pl.pallas_call(
        functools.partial(_even_kernel, ret_log_decay=_retention_log_decays()),
        grid=(batch, nt),
        in_specs=[
            pl.BlockSpec((ts, d), row),
            pl.BlockSpec((1, d), c2),
            pl.BlockSpec(w_m.shape, c2),
            pl.BlockSpec(w_r.shape, c2),
            pl.BlockSpec(w_gate_c.shape, c2),
            pl.BlockSpec(w_gate_r.shape, c2),
            pl.BlockSpec(gate_b_c.shape, c2),
            pl.BlockSpec(gate_b_r.shape, c2),
            pl.BlockSpec(qk_conv_w.shape, c2),
            pl.BlockSpec((1, 2 * width), c2),
            pl.BlockSpec((ts, HEAD_DIM), seq_row),
            pl.BlockSpec((ts, HEAD_DIM), seq_row),
            pl.BlockSpec(mlstm_norm_g.shape, c2),
            pl.BlockSpec(ret_norm_g.shape, c2),
            pl.BlockSpec(w_out.shape, c2),
        ],
        out_specs=pl.BlockSpec((ts, d), row),
        out_shape=jax.ShapeDtypeStruct((n_tok, d), F32),
        scratch_shapes=[
            pltpu.VMEM((ts + QK_HALO, 8 * width), F32),
            pltpu.VMEM((ts, 2 * width), F32),
            pltpu.VMEM((ts, 2 * width), BF16),
            pltpu.VMEM((N_PAIRS, PAIR_WIDTH, PAIR_WIDTH), F32),
            pltpu.VMEM((N_PAIRS, 1, PAIR_WIDTH), F32),
            pltpu.VMEM((N_HEADS, 1, HEAD_DIM), F32),
            pltpu.VMEM((N_PAIRS, PAIR_WIDTH, PAIR_WIDTH), F32),
        ],
        compiler_params=pltpu.CompilerParams(
            dimension_semantics=("arbitrary", "arbitrary"), vmem_limit_bytes=VMEM_LIMIT_BYTES),
        name="even_mixer",
    )(x2, norm_g.reshape(1, d), w_m, w_r, w_gate_c, w_gate_r, gate_b_c, gate_b_r,
      qk_conv_w, qk_conv_b.reshape(1, 2 * width), jnp.asarray(cos_t), jnp.asarray(sin_t),
      mlstm_norm_g, ret_norm_g, w_out.astype(BF16))


def kernel(x, norm_mix_g, norm_ffn_g, final_norm_g, ev_w_in, ev_qk_conv_w, ev_qk_conv_b, ev_i_bias, ev_f_bias, ev_mlstm_norm_g, ev_ret_norm_g, ev_w_out, od_w_in, od_pool_w, od_pool_scale, od_dw_w, od_dw_b, od_conv_norm_g, od_conv_norm_b, od_w_out, ffn_w_gate, ffn_w_up, ffn_w_down):
    batch, seq, d = x.shape
    depth = norm_mix_g.shape[0]
    x2 = x.reshape(batch * seq, d)
    w_gate, w_up, w_down = (w.astype(BF16) for w in (ffn_w_gate, ffn_w_up, ffn_w_down))
    for layer in range(depth):
        j = layer // 2
        last = layer == depth - 1
        if layer % 2 == 0:
            x2 = _even_mixer(x2, batch, seq, norm_mix_g[layer], ev_w_in[j], ev_qk_conv_w[j],
                             ev_qk_conv_b[j], ev_i_bias[j], ev_f_bias[j], ev_mlstm_norm_g[j],
                             ev_ret_norm_g[j], ev_w_out[j])
            x2 = _ffn(x2, layer, norm_ffn_g[layer], w_gate, w_up, w_down, final_norm_g,
                      apply_final_norm=last)
        else:
            x2 = _odd_layer(x2, batch, seq, norm_mix_g[layer], od_w_in[j], od_pool_w[j],
                            od_pool_scale[j], od_dw_w[j], od_dw_b[j], od_conv_norm_g[j],
                            od_conv_norm_b[j], od_w_out[j], norm_ffn_g[layer], layer, w_gate,
                            w_up, w_down, final_norm_g, apply_final_norm=last)
    return x2.reshape(batch, seq, d)
```

```python
import functools
import math

import numpy as np
import jax
import jax.numpy as jnp
from jax import lax
from jax.experimental import pallas as pl
from jax.experimental.pallas import tpu as pltpu

F32 = jnp.float32
BF16 = jnp.bfloat16

EPS = 1e-6
NEG = -1e30
CHUNK = 128
HEAD_DIM = 128
N_HEADS = 4
QK_CONV = 4
ROPE_BASE = 10000.0
POOL_WINDOWS = (2, 4, 8, 16)
POOL_GROUP = 128
DW_KERNEL = 31

SEQ_TILE = 512
FFN_TILE = 1024
FF_CHUNK = 256
QK_HALO = 8
POOL_HALO = 16
DW_HALO = 32
DW_ROWS = 64
VMEM_LIMIT_BYTES = 56 * 1024 * 1024


def _rmsnorm(x, g):
    ms = jnp.mean(x * x, axis=-1, keepdims=True)
    return (x * lax.rsqrt(ms + EPS)) * g


def _normalize(x):
    xc = x - jnp.mean(x, axis=-1, keepdims=True)
    return xc * lax.rsqrt(jnp.mean(xc * xc, axis=-1, keepdims=True) + EPS)


def _sigmoid(x):
    return 1.0 / (1.0 + jnp.exp(-x))


def _silu(x):
    return x * _sigmoid(x)


def _log_sigmoid(x):
    return jnp.minimum(x, 0.0) - jnp.log(1.0 + jnp.exp(-jnp.abs(x)))


def _dot(a, b):
    return jnp.dot(a.astype(BF16), b.astype(BF16), preferred_element_type=F32)


def _dot_nt(a, b):
    return lax.dot_general(a.astype(BF16), b.astype(BF16), (((1,), (1,)), ((), ())),
                           preferred_element_type=F32)


def _dot_tn(a, b):
    return lax.dot_general(a.astype(BF16), b.astype(BF16), (((0,), (0,)), ((), ())),
                           preferred_element_type=F32)


def _ffn_kernel(x_ref, g_ref, wg_ref, wu_ref, wd_ref, fg_ref, o_ref, *, apply_final_norm):
    x = x_ref[...]
    h = _rmsnorm(x, g_ref[...]).astype(BF16)
    acc = x
    d_ff = wg_ref.shape[1]
    for c0 in range(0, d_ff, FF_CHUNK):
        cols = slice(c0, c0 + FF_CHUNK)
        gate = jnp.dot(h, wg_ref[:, cols], preferred_element_type=F32)
        up = jnp.dot(h, wu_ref[:, cols], preferred_element_type=F32)
        a = (_silu(gate) * up).astype(BF16)
        acc = acc + jnp.dot(a, wd_ref[cols, :], preferred_element_type=F32)
    if apply_final_norm:
        acc = _rmsnorm(acc, fg_ref[...])
    o_ref[...] = acc


def _ffn(x2, layer, norm_g, w_gate, w_up, w_down, final_g, apply_final_norm):
    n_tok, d = x2.shape
    d_ff = w_gate.shape[2]
    assert d_ff % FF_CHUNK == 0 and n_tok % FFN_TILE == 0
    const2 = lambda i: (0, 0)
    pick = lambda i: (layer, 0, 0)
    return pl.pallas_call(
        functools.partial(_ffn_kernel, apply_final_norm=apply_final_norm),
        grid=(n_tok // FFN_TILE,),
        in_specs=[
            pl.BlockSpec((FFN_TILE, d), lambda i: (i, 0)),
            pl.BlockSpec((1, d), const2),
            pl.BlockSpec((None, d, d_ff), pick, pipeline_mode=pl.Buffered(1)),
            pl.BlockSpec((None, d, d_ff), pick, pipeline_mode=pl.Buffered(1)),
            pl.BlockSpec((None, d_ff, d), pick, pipeline_mode=pl.Buffered(1)),
            pl.BlockSpec((1, d), const2),
        ],
        out_specs=pl.BlockSpec((FFN_TILE, d), lambda i: (i, 0)),
        out_shape=jax.ShapeDtypeStruct((n_tok, d), F32),
        compiler_params=pltpu.CompilerParams(
            dimension_semantics=("arbitrary",), vmem_limit_bytes=VMEM_LIMIT_BYTES),
        name="ffn",
    )(x2, norm_g.reshape(1, d), w_gate, w_up, w_down, final_g.reshape(1, d))


def _interleave(main, fill):
    n = len(main)
    for s, fn in enumerate(main):
        fn()
        for g in fill[s * len(fill) // n:(s + 1) * len(fill) // n]:
            g()


def _odd_layer_kernel(x_ref, g_ref, win_ref, poolw_ref, pscale_ref, dww_ref, dwb_ref, cng_ref,
                      cnb_ref, wout_ref, fg_ref, wg_ref, wu_ref, wd_ref, final_g_ref, o_ref,
                      ubuf, ybuf, yrot, cbuf, pool_out, xmid,
                      *, pool_width, tiles_per_seq, n_tiles, apply_final_norm):
    s = pl.program_id(0)
    ts = x_ref.shape[0]
    conv_width = ybuf.shape[1]
    d_ff = wg_ref.shape[1]
    seq_tile = s % tiles_per_seq

    @pl.when(seq_tile == 0)
    def _():
        ubuf[0:POOL_HALO, :] = jnp.zeros((POOL_HALO, ubuf.shape[1]), F32)
        ybuf[0:DW_HALO, :] = jnp.zeros((DW_HALO, ybuf.shape[1]), F32)

    @pl.when(seq_tile != 0)
    def _():
        ubuf[0:POOL_HALO, :] = ubuf[ts:ts + POOL_HALO, :]
        ybuf[0:DW_HALO, :] = ybuf[ts:ts + DW_HALO, :]

    @pl.when(s >= 1)
    def _():
        yf = _normalize(cbuf[...]) * cng_ref[...] + cnb_ref[...]
        xmid[...] = pool_out[...] + _dot(_silu(yf), wout_ref[pool_width:, :])

    pool_st = {}

    def pool(g, w):
        cols = slice(g * POOL_GROUP, (g + 1) * POOL_GROUP)
        pos = (seq_tile * ts + lax.broadcasted_iota(jnp.int32, (ts, 1), 0)).astype(F32) + 1.0
        wsum = ubuf[:, cols]
        shift = 1
        while shift < w:
            wsum = wsum + pltpu.roll(wsum, shift, axis=0)
            shift *= 2
        cur = ubuf[POOL_HALO:POOL_HALO + ts, cols]
        pooled = wsum[POOL_HALO:, :] / jnp.minimum(pos, float(w)) - cur
        y_pool = _dot(pooled, poolw_ref[g]) * pscale_ref[:, cols]
        prev = x_ref[...] if g == 0 else pool_st["out"]
        pool_st["out"] = prev + _dot(y_pool, wout_ref[cols, :])
        if g == len(POOL_WINDOWS) - 1:
            pool_out[...] = pool_st["out"]

    @pl.when(s < n_tiles)
    def _():
        h = _rmsnorm(x_ref[...], g_ref[...]).astype(BF16)
        z = jnp.dot(h, win_ref[...], preferred_element_type=F32)
        ubuf[POOL_HALO:POOL_HALO + ts, :] = z[:, :pool_width]
        ga = z[:, pool_width:pool_width + conv_width]
        gb = z[:, pool_width + conv_width:]
        ybuf[DW_HALO:DW_HALO + ts, :] = ga * _sigmoid(gb)
        for g, w in enumerate(POOL_WINDOWS):
            pool(g, w)

    rot_rows = yrot.shape[1]

    def shifted_copy(r):
        yrot[r - 1] = ybuf[r:r + rot_rows, :]

    def conv(r0, c0):
        cols = slice(c0, c0 + 128)
        acc = jnp.broadcast_to(dwb_ref[:, cols], (DW_ROWS, 128))
        for k in range(DW_KERNEL):
            start = DW_HALO - (DW_KERNEL - 1) + k + r0
            src = ybuf if start % 8 == 0 else yrot.at[start % 8 - 1]
            aligned = start - start % 8
            acc = acc + dww_ref[k:k + 1, cols] * src[aligned:aligned + DW_ROWS, cols]
        cbuf[r0:r0 + DW_ROWS, cols] = acc

    vector_pieces = [functools.partial(shifted_copy, r) for r in range(1, 8)]
    vector_pieces += [functools.partial(conv, r0, c0) for r0 in range(0, ts, DW_ROWS)
                      for c0 in range(0, conv_width, 128)]

    st = {}

    def ffn_start():
        st["acc"] = xmid[...]
        st["h"] = _rmsnorm(st["acc"], fg_ref[...]).astype(BF16)

    def ffn_gate(c0):
        st["gate"] = jnp.dot(st["h"], wg_ref[:, c0:c0 + FF_CHUNK], preferred_element_type=F32)

    def ffn_up(c0):
        up = jnp.dot(st["h"], wu_ref[:, c0:c0 + FF_CHUNK], preferred_element_type=F32)
        st["a"] = (_silu(st["gate"]) * up).astype(BF16)

    def ffn_down(c0):
        st["acc"] = st["acc"] + jnp.dot(st["a"], wd_ref[c0:c0 + FF_CHUNK, :],
                                        preferred_element_type=F32)

    def ffn_finish():
        acc = st["acc"]
        if apply_final_norm:
            acc = _rmsnorm(acc, final_g_ref[...])
        o_ref[...] = acc

    matmul_pieces = [ffn_start]
    for c0 in range(0, d_ff, FF_CHUNK):
        matmul_pieces += [functools.partial(f, c0) for f in (ffn_gate, ffn_up, ffn_down)]
    matmul_pieces += [ffn_finish]

    @pl.when(s == 0)
    def _():
        for piece in vector_pieces:
            piece()

    @pl.when(s >= 1)
    def _():
        _interleave(matmul_pieces, vector_pieces)


def _odd_layer(x2, batch, seq, norm_g, w_in, pool_w, pool_scale, dw_w, dw_b, cn_g, cn_b, w_out,
               ffn_norm_g, layer, w_gate, w_up, w_down, final_g, apply_final_norm):
    n_tok, d = x2.shape
    pool_width = pool_w.shape[0] * pool_w.shape[1]
    conv_width = dw_w.shape[1]
    d_ff = w_gate.shape[2]
    ts = SEQ_TILE
    nt = seq // ts
    n_tiles = batch * nt
    assert nt * ts == seq and ts % DW_ROWS == 0 and d_ff % FF_CHUNK == 0
    c2 = lambda s: (0, 0)
    c3 = lambda s: (0, 0, 0)
    pick = lambda s: (layer, 0, 0)
    resident = functools.partial(pl.BlockSpec, pipeline_mode=pl.Buffered(1))
    return pl.pallas_call(
        functools.partial(_odd_layer_kernel, pool_width=pool_width, tiles_per_seq=nt,
                          n_tiles=n_tiles, apply_final_norm=apply_final_norm),
        grid=(n_tiles + 1,),
        in_specs=[
            pl.BlockSpec((ts, d), lambda s: (jnp.minimum(s, n_tiles - 1), 0)),
            pl.BlockSpec((1, d), c2),
            resident(w_in.shape, c2),
            pl.BlockSpec(pool_w.shape, c3),
            pl.BlockSpec((1, pool_width), c2),
            pl.BlockSpec(dw_w.shape, c2),
            pl.BlockSpec((1, conv_width), c2),
            pl.BlockSpec((1, conv_width), c2),
            pl.BlockSpec((1, conv_width), c2),
            resident(w_out.shape, c2),
            pl.BlockSpec((1, d), c2),
            resident((None, d, d_ff), pick),
            resident((None, d, d_ff), pick),
            resident((None, d_ff, d), pick),
            pl.BlockSpec((1, d), c2),
        ],
        out_specs=pl.BlockSpec((ts, d), lambda s: (jnp.maximum(s - 1, 0), 0)),
        out_shape=jax.ShapeDtypeStruct((n_tok, d), F32),
        scratch_shapes=[
            pltpu.VMEM((ts + POOL_HALO, pool_width), F32),
            pltpu.VMEM((ts + DW_HALO, conv_width), F32),
            pltpu.VMEM((7, ts + DW_HALO - 8, conv_width), F32),
            pltpu.VMEM((ts, conv_width), F32),
            pltpu.VMEM((ts, d), F32),
            pltpu.VMEM((ts, d), F32),
        ],
        compiler_params=pltpu.CompilerParams(
            dimension_semantics=("arbitrary",), vmem_limit_bytes=VMEM_LIMIT_BYTES),
        name="odd_layer",
    )(x2, norm_g.reshape(1, d), w_in.astype(BF16), pool_w.astype(BF16),
      pool_scale.reshape(1, pool_width), dw_w, dw_b.reshape(1, conv_width),
      cn_g.reshape(1, conv_width), cn_b.reshape(1, conv_width), w_out.astype(BF16),
      ffn_norm_g.reshape(1, d), w_gate, w_up, w_down, final_g.reshape(1, d))


PAIR_WIDTH = 2 * HEAD_DIM
N_PAIRS = N_HEADS // 2
PAIR_COLS = 4 * PAIR_WIDTH


def _chunk_cumsum_rows(x, n):
    idx = lax.broadcasted_iota(jnp.int32, x.shape, 0) % n
    s = 1
    while s < n:
        x = x + jnp.where(idx >= s, pltpu.roll(x, s, axis=0), 0.0)
        s *= 2
    return x


def _chunk_cumsum_lanes(x, n):
    idx = lax.broadcasted_iota(jnp.int32, x.shape, 1) % n
    s = 1
    while s < n:
        x = x + jnp.where(idx >= s, pltpu.roll(x, s, axis=1), 0.0)
        s *= 2
    return x


def _halves(a2):
    return a2[:, :HEAD_DIM], a2[:, HEAD_DIM:]


def _pair(a0, a1):
    return jnp.concatenate([a0, a1], axis=1)


def _block_diag(a2):
    a0, a1 = _halves(a2)
    z = jnp.zeros_like(a0)
    return jnp.concatenate([_pair(a0, z), _pair(z, a1)], axis=0)


def _pair_head_norm(x2, g2):
    x0, x1 = _halves(x2)
    return _pair(_normalize(x0), _normalize(x1)) * g2


def _rotary_pair(t2, cos, sin):
    t0, t1 = _halves(t2)
    return _pair(t0 * cos + pltpu.roll(t0, HEAD_DIM // 2, axis=1) * sin,
                 t1 * cos + pltpu.roll(t1, HEAD_DIM // 2, axis=1) * sin)


def _even_kernel(x_ref, g_ref, wm_ref, wr_ref, wgc_ref, wgr_ref, gbc_ref, gbr_ref, cw_ref, cb_ref,
                 cos_ref, sin_ref, mng_ref, rng_ref, wout_ref, o_ref,
                 zbuf, qkbuf, mixbuf, c_state, n_state, m_state, r_state, *, ret_log_decay):
    i = pl.program_id(1)
    ts = x_ref.shape[0]
    scale = HEAD_DIM ** -0.5
    n_chunks = ts // CHUNK

    @pl.when(i == 0)
    def _():
        zbuf[0:QK_HALO, :] = jnp.zeros((QK_HALO, zbuf.shape[1]), F32)
        c_state[...] = jnp.zeros(c_state.shape, F32)
        n_state[...] = jnp.zeros(n_state.shape, F32)
        m_state[...] = jnp.full(m_state.shape, NEG, F32)
        r_state[...] = jnp.zeros(r_state.shape, F32)

    x = x_ref[...]
    h = _rmsnorm(x, g_ref[...]).astype(BF16)
    width = N_HEADS * HEAD_DIM
    for pp in range(2 * N_PAIRS):
        w_ref = (wm_ref, wr_ref)[pp // N_PAIRS]
        for j in range(PAIR_COLS // PAIR_WIDTH):
            src = j * width + (pp % N_PAIRS) * PAIR_WIDTH
            dst = pp * PAIR_COLS + j * PAIR_WIDTH
            zbuf[QK_HALO:QK_HALO + ts, dst:dst + PAIR_WIDTH] = jnp.dot(
                h, w_ref[:, src:src + PAIR_WIDTH], preferred_element_type=F32)

    gate_c = jnp.dot(h, wgc_ref[...], preferred_element_type=F32) + gbc_ref[...]
    gate_r = _dot_nt(wgr_ref[...], h) + gbr_ref[...]
    ig_c = gate_c[:, 0:N_HEADS]
    b_c = _chunk_cumsum_rows(_log_sigmoid(gate_c), CHUNK)[:, N_HEADS:2 * N_HEADS]
    ig_r = gate_r[0:N_HEADS, :]
    b_r = _chunk_cumsum_lanes(_log_sigmoid(gate_r), CHUNK)[N_HEADS:2 * N_HEADS, :]

    row_i = lax.broadcasted_iota(jnp.int32, (CHUNK, CHUNK), 0)
    col_j = lax.broadcasted_iota(jnp.int32, (CHUNK, CHUNK), 1)
    causal = col_j <= row_i
    rel = (row_i - col_j).astype(F32)
    pos_c = lax.broadcasted_iota(jnp.int32, (CHUNK, 1), 0).astype(F32)
    blk_r = lax.broadcasted_iota(jnp.int32, (PAIR_WIDTH, PAIR_WIDTH), 0) // HEAD_DIM
    blk_c = lax.broadcasted_iota(jnp.int32, (PAIR_WIDTH, PAIR_WIDTH), 1) // HEAD_DIM
    same_head = blk_r == blk_c
    ones_row = jnp.ones((1, HEAD_DIM), F32)

    for p in range(N_PAIRS):
        base = p * PAIR_COLS
        qk_cols = slice(p * 2 * PAIR_WIDTH, (p + 1) * 2 * PAIR_WIDTH)
        q_cols = slice(p * PAIR_WIDTH, (p + 1) * PAIR_WIDTH)
        k_cols = slice(width + p * PAIR_WIDTH, width + (p + 1) * PAIR_WIDTH)
        conv = jnp.broadcast_to(_pair(cb_ref[:, q_cols], cb_ref[:, k_cols]), (ts, 2 * PAIR_WIDTH))
        for k in range(QK_CONV):
            start = QK_HALO - (QK_CONV - 1) + k
            w_k = _pair(cw_ref[k:k + 1, q_cols], cw_ref[k:k + 1, k_cols])
            conv = conv + w_k * zbuf[start:start + ts, base:base + 2 * PAIR_WIDTH]
        qkbuf[:, qk_cols] = _silu(conv)

    def mlstm_a(p, c, st):
        base = p * PAIR_COLS
        rows = slice(c * CHUNK, (c + 1) * CHUNK)
        zrows = slice(QK_HALO + c * CHUNK, QK_HALO + (c + 1) * CHUNK)
        q2 = qkbuf[rows, p * 2 * PAIR_WIDTH: p * 2 * PAIR_WIDTH + PAIR_WIDTH]
        k2 = qkbuf[rows, p * 2 * PAIR_WIDTH + PAIR_WIDTH: (p + 1) * 2 * PAIR_WIDTH] * scale
        v2 = zbuf[zrows, base + 2 * PAIR_WIDTH: base + 3 * PAIR_WIDTH]
        k_h = _halves(k2)
        log_d, m, inter, a_end, s_old, s_new, m_new = [], [], [], [], [], [], []
        for e in range(2):
            hd = 2 * p + e
            m_prev = st["m"][e]
            bc = b_c[rows, hd:hd + 1]
            ic = ig_c[rows, hd:hd + 1]
            br = b_r[hd:hd + 1, rows]
            ir = ig_r[hd:hd + 1, rows]
            ld = jnp.where(causal, bc - br + ir, NEG)
            m_intra = jnp.max(ld, axis=1, keepdims=True)
            total = bc[CHUNK - 1:CHUNK, :]
            w_end = total - bc + ic
            m_loc = jnp.max(w_end, axis=0, keepdims=True)
            log_inter = bc + m_prev
            m_e = jnp.maximum(m_intra, log_inter)
            mn = jnp.maximum(total + m_prev, m_loc)
            log_d.append(ld - m_e)
            m.append(m_e)
            inter.append(jnp.exp(log_inter - m_e))
            a_end.append(jnp.exp(w_end - m_loc))
            s_old.append(jnp.exp(total + m_prev - mn))
            s_new.append(jnp.exp(m_loc - mn))
            m_new.append(mn)
        ak2 = _pair(a_end[0] * k_h[0], a_end[1] * k_h[1])
        return dict(rows=rows, zrows=zrows, q2=q2, v2=v2, m=m, inter=inter,
                    decay=jnp.exp(_pair(log_d[0], log_d[1])),
                    qk=_dot_nt(q2, _block_diag(k2)),
                    q_state=_dot(q2, st["c"]),
                    kv=_dot_tn(ak2, v2),
                    n_sum=jnp.sum(ak2, axis=0, keepdims=True),
                    s_old2=_pair(s_old[0] * ones_row, s_old[1] * ones_row),
                    s_new2=_pair(s_new[0] * ones_row, s_new[1] * ones_row),
                    m_new=m_new)

    def mlstm_c(p, st, t):
        base = p * PAIR_COLS
        og2 = zbuf[t["zrows"], base + 3 * PAIR_WIDTH: base + 4 * PAIR_WIDTH]
        s_h = _halves(t["s2"])
        q_h = _halves(t["q2"])
        n_h = _halves(st["n"])
        hm = []
        for e in range(2):
            num = _halves(t["sv"])[e] + t["inter"][e] * _halves(t["q_state"])[e]
            den = (jnp.sum(s_h[e], axis=1, keepdims=True)
                   + t["inter"][e] * jnp.sum(q_h[e] * n_h[e], axis=1, keepdims=True))
            hm.append(num / jnp.maximum(jnp.abs(den), jnp.exp(-t["m"][e])))
        g2 = _pair(mng_ref[2 * p:2 * p + 1, :], mng_ref[2 * p + 1:2 * p + 2, :])
        hm2 = _pair_head_norm(_sigmoid(og2) * _pair(hm[0], hm[1]), g2)
        mixbuf[t["rows"], p * PAIR_WIDTH:(p + 1) * PAIR_WIDTH] = hm2.astype(BF16)
        st["c"] = t["s_old2"] * st["c"] + t["s_new2"] * jnp.where(same_head, t["kv"], 0.0)
        st["n"] = t["s_old2"] * st["n"] + t["s_new2"] * t["n_sum"]
        st["m"] = t["m_new"]

    def ret_a(p, c, st):
        base = (N_PAIRS + p) * PAIR_COLS
        rows = slice(c * CHUNK, (c + 1) * CHUNK)
        zrows = slice(QK_HALO + c * CHUNK, QK_HALO + (c + 1) * CHUNK)
        cos = cos_ref[rows, :]
        sin = sin_ref[rows, :]
        q2 = _rotary_pair(zbuf[zrows, base:base + PAIR_WIDTH], cos, sin)
        k2 = _rotary_pair(zbuf[zrows, base + PAIR_WIDTH:base + 2 * PAIR_WIDTH], cos, sin) * scale
        v2 = zbuf[zrows, base + 2 * PAIR_WIDTH: base + 3 * PAIR_WIDTH]
        return dict(rows=rows, zrows=zrows, v2=v2, decay=st["decay2"],
                    qk=_dot_nt(q2, _block_diag(k2)),
                    q_state=_dot(q2, st["r"]),
                    kv=_dot_tn(st["w_end2"] * k2, v2))

    def ret_c(p, st, t):
        base = (N_PAIRS + p) * PAIR_COLS
        rg2 = zbuf[t["zrows"], base + 3 * PAIR_WIDTH: base + 4 * PAIR_WIDTH]
        o2 = t["sv"] + t["q_state"] * st["q_decay2"]
        g2 = _pair(rng_ref[2 * p:2 * p + 1, :], rng_ref[2 * p + 1:2 * p + 2, :])
        hr2 = _silu(rg2) * _pair_head_norm(o2, g2)
        mixbuf[t["rows"], (N_PAIRS + p) * PAIR_WIDTH:(N_PAIRS + p + 1) * PAIR_WIDTH] = hr2.astype(BF16)
        st["r"] = st["chunk_decay2"] * st["r"] + jnp.where(same_head, t["kv"], 0.0)

    def stage_b(t):
        t["s2"] = t["qk"] * t["decay"]
        t["sv"] = _dot(t["s2"], _block_diag(t["v2"]))

    lanes = []
    for p in range(N_PAIRS):
        lanes.append((functools.partial(mlstm_a, p), functools.partial(mlstm_c, p),
                      dict(c=c_state[p], n=n_state[p],
                           m=[m_state[2 * p + e][:, 0:1] for e in range(2)])))
    for p in range(N_PAIRS):
        decay, w_end_r, q_decay, chunk_decay = [], [], [], []
        for e in range(2):
            lg = ret_log_decay[2 * p + e]
            decay.append(jnp.where(causal, jnp.exp(lg * jnp.maximum(rel, 0.0)), 0.0))
            w_end_r.append(jnp.exp(lg * (CHUNK - 1.0 - pos_c)) * ones_row)
            q_decay.append(jnp.exp(lg * (pos_c + 1.0)) * ones_row)
            chunk_decay.append(math.exp(lg * CHUNK) * ones_row)
        lanes.append((functools.partial(ret_a, p), functools.partial(ret_c, p),
                      dict(r=r_state[p], decay2=_pair(*decay), w_end2=_pair(*w_end_r),
                           q_decay2=_pair(*q_decay), chunk_decay2=_pair(*chunk_decay))))

    for c in range(n_chunks):
        work = [stage_a(c, st) for stage_a, _, st in lanes]
        for t in work:
            stage_b(t)
        for (_, stage_c, st), t in zip(lanes, work):
            stage_c(st, t)

    for p in range(N_PAIRS):
        st = lanes[p][2]
        c_state[p] = st["c"]
        n_state[p] = st["n"]
        for e in range(2):
            m_state[2 * p + e] = st["m"][e] * ones_row
        r_state[p] = lanes[N_PAIRS + p][2]["r"]

    o_ref[...] = x + jnp.dot(mixbuf[...], wout_ref[...], preferred_element_type=F32)
    zbuf[0:QK_HALO, :] = zbuf[ts:ts + QK_HALO, :]


def _retention_log_decays():
    h = np.arange(N_HEADS, dtype=np.float32)
    lg = np.log(np.float32(1.0) - np.float32(2.0) ** (np.float32(-5.0) - h)).astype(np.float32)
    return tuple(float(v) for v in lg)


@functools.lru_cache(maxsize=None)
def _rotary_tables(seq):
    inv = np.float32(ROPE_BASE) ** (-np.arange(0, HEAD_DIM, 2, dtype=np.float32) / np.float32(HEAD_DIM))
    ang = np.arange(seq, dtype=np.float32)[:, None] * inv[None, :]
    cos = np.cos(ang).astype(np.float32)
    sin = np.sin(ang).astype(np.float32)
    return np.concatenate([cos, cos], axis=-1), np.concatenate([-sin, sin], axis=-1)


def _even_mixer(x2, batch, seq, norm_g, w_in, qk_conv_w, qk_conv_b, i_bias, f_bias,
                mlstm_norm_g, ret_norm_g, w_out):
    n_tok, d = x2.shape
    width = N_HEADS * HEAD_DIM
    ts = SEQ_TILE
    nt = seq // ts
    assert nt * ts == seq and ts % CHUNK == 0
    g0 = 4 * width
    g1 = g0 + 2 * N_HEADS
    w_m = w_in[:, :g0].astype(BF16)
    w_r = w_in[:, g1:].astype(BF16)
    w_gate = w_in[:, g0:g1].astype(BF16)
    w_gate_c = jnp.pad(w_gate, ((0, 0), (0, HEAD_DIM - 2 * N_HEADS)))
    w_gate_r = w_gate.T
    gate_b = jnp.concatenate([i_bias, f_bias]).astype(F32)
    gate_b_c = jnp.pad(gate_b, (0, HEAD_DIM - 2 * N_HEADS)).reshape(1, HEAD_DIM)
    gate_b_r = gate_b.reshape(2 * N_HEADS, 1)
    cos_t, sin_t = _rotary_tables(seq)

    row = lambda b, i: (b * nt + i, 0)
    seq_row = lambda b, i: (i, 0)
    c2 = lambda b, i: (0, 0)
    return pl.pallas_call(
        functools.partial(_even_kernel, ret_log_decay=_retention_log_decays()),
        grid=(batch, nt),
        in_specs=[
            pl.BlockSpec((ts, d), row),
            pl.BlockSpec((1, d), c2),
            pl.BlockSpec(w_m.shape, c2),
            pl.BlockSpec(w_r.shape, c2),
            pl.BlockSpec(w_gate_c.shape, c2),
            pl.BlockSpec(w_gate_r.shape, c2),
            pl.BlockSpec(gate_b_c.shape, c2),
            pl.BlockSpec(gate_b_r.shape, c2),
            pl.BlockSpec(qk_conv_w.shape, c2),
            pl.BlockSpec((1, 2 * width), c2),
            pl.BlockSpec((ts, HEAD_DIM), seq_row),
            pl.BlockSpec((ts, HEAD_DIM), seq_row),
            pl.BlockSpec(mlstm_norm_g.shape, c2),
            pl.BlockSpec(ret_norm_g.shape, c2),
            pl.BlockSpec(w_out.shape, c2),
        ],
        out_specs=pl.BlockSpec((ts, d), row),
        out_shape=jax.ShapeDtypeStruct((n_tok, d), F32),
        scratch_shapes=[
            pltpu.VMEM((ts + QK_HALO, 8 * width), F32),
            pltpu.VMEM((ts, 2 * width), F32),
            pltpu.VMEM((ts, 2 * width), BF16),
            pltpu.VMEM((N_PAIRS, PAIR_WIDTH, PAIR_WIDTH), F32),
            pltpu.VMEM((N_PAIRS, 1, PAIR_WIDTH), F32),
            pltpu.VMEM((N_HEADS, 1, HEAD_DIM), F32),
            pltpu.VMEM((N_PAIRS, PAIR_WIDTH, PAIR_WIDTH), F32),
        ],
        compiler_params=pltpu.CompilerParams(
            dimension_semantics=("arbitrary", "arbitrary"), vmem_limit_bytes=VMEM_LIMIT_BYTES),
        name="even_mixer",
    )(x2, norm_g.reshape(1, d), w_m, w_r, w_gate_c, w_gate_r, gate_b_c, gate_b_r,
      qk_conv_w, qk_conv_b.reshape(1, 2 * width), jnp.asarray(cos_t), jnp.asarray(sin_t),
      mlstm_norm_g, ret_norm_g, w_out.astype(BF16))


def kernel(x, norm_mix_g, norm_ffn_g, final_norm_g, ev_w_in, ev_qk_conv_w, ev_qk_conv_b, ev_i_bias, ev_f_bias, ev_mlstm_norm_g, ev_ret_norm_g, ev_w_out, od_w_in, od_pool_w, od_pool_scale, od_dw_w, od_dw_b, od_conv_norm_g, od_conv_norm_b, od_w_out, ffn_w_gate, ffn_w_up, ffn_w_down):
    batch, seq, d = x.shape
    depth = norm_mix_g.shape[0]
    x2 = x.reshape(batch * seq, d)
    w_gate, w_up, w_down = (w.astype(BF16) for w in (ffn_w_gate, ffn_w_up, ffn_w_down))
    for layer in range(depth):
        j = layer // 2
        last = layer == depth - 1
        if layer % 2 == 0:
            x2 = _even_mixer(x2, batch, seq, norm_mix_g[layer], ev_w_in[j], ev_qk_conv_w[j],
                             ev_qk_conv_b[j], ev_i_bias[j], ev_f_bias[j], ev_mlstm_norm_g[j],
                             ev_ret_norm_g[j], ev_w_out[j])
            x2 = _ffn(x2, layer, norm_ffn_g[layer], w_gate, w_up, w_down, final_norm_g,
                      apply_final_norm=last)
        else:
            x2 = _odd_layer(x2, batch, seq, norm_mix_g[layer], od_w_in[j], od_pool_w[j],
                            od_pool_scale[j], od_dw_w[j], od_dw_b[j], od_conv_norm_g[j],
                            od_conv_norm_b[j], od_w_out[j], norm_ffn_g[layer], layer, w_gate,
                            w_up, w_down, final_norm_g, apply_final_norm=last)
    return x2.reshape(batch, seq, d)
```

```python
import functools
import math

import numpy as np
import jax
import jax.numpy as jnp
from jax import lax
from jax.experimental import pallas as pl
from jax.experimental.pallas import tpu as pltpu

F32 = jnp.float32
BF16 = jnp.bfloat16

EPS = 1e-6
NEG = -1e30
CHUNK = 128
HEAD_DIM = 128
N_HEADS = 4
QK_CONV = 4
ROPE_BASE = 10000.0
POOL_WINDOWS = (2, 4, 8, 16)
POOL_GROUP = 128
DW_KERNEL = 31

SEQ_TILE = 512
FFN_TILE = 1024
FF_CHUNK = 256
QK_HALO = 8
POOL_HALO = 16
DW_HALO = 32
DW_ROWS = 64
VMEM_LIMIT_BYTES = 56 * 1024 * 1024


def _rmsnorm(x, g):
    ms = jnp.mean(x * x, axis=-1, keepdims=True)
    return (x * lax.rsqrt(ms + EPS)) * g


def _normalize(x):
    xc = x - jnp.mean(x, axis=-1, keepdims=True)
    return xc * lax.rsqrt(jnp.mean(xc * xc, axis=-1, keepdims=True) + EPS)


def _sigmoid(x):
    return 1.0 / (1.0 + jnp.exp(-x))


def _silu(x):
    return x * _sigmoid(x)


def _log_sigmoid(x):
    return jnp.minimum(x, 0.0) - jnp.log(1.0 + jnp.exp(-jnp.abs(x)))


def _dot(a, b):
    return jnp.dot(a.astype(BF16), b.astype(BF16), preferred_element_type=F32)


def _dot_nt(a, b):
    return lax.dot_general(a.astype(BF16), b.astype(BF16), (((1,), (1,)), ((), ())),
                           preferred_element_type=F32)


def _dot_tn(a, b):
    return lax.dot_general(a.astype(BF16), b.astype(BF16), (((0,), (0,)), ((), ())),
                           preferred_element_type=F32)


def _ffn_kernel(x_ref, g_ref, wg_ref, wu_ref, wd_ref, fg_ref, o_ref, *, apply_final_norm):
    x = x_ref[...]
    h = _rmsnorm(x, g_ref[...]).astype(BF16)
    acc = x
    d_ff = wg_ref.shape[1]
    for c0 in range(0, d_ff, FF_CHUNK):
        cols = slice(c0, c0 + FF_CHUNK)
        gate = jnp.dot(h, wg_ref[:, cols], preferred_element_type=F32)
        up = jnp.dot(h, wu_ref[:, cols], preferred_element_type=F32)
        a = (_silu(gate) * up).astype(BF16)
        acc = acc + jnp.dot(a, wd_ref[cols, :], preferred_element_type=F32)
    if apply_final_norm:
        acc = _rmsnorm(acc, fg_ref[...])
    o_ref[...] = acc


def _ffn(x2, layer, norm_g, w_gate, w_up, w_down, final_g, apply_final_norm):
    n_tok, d = x2.shape
    d_ff = w_gate.shape[2]
    assert d_ff % FF_CHUNK == 0 and n_tok % FFN_TILE == 0
    const2 = lambda i: (0, 0)
    pick = lambda i: (layer, 0, 0)
    return pl.pallas_call(
        functools.partial(_ffn_kernel, apply_final_norm=apply_final_norm),
        grid=(n_tok // FFN_TILE,),
        in_specs=[
            pl.BlockSpec((FFN_TILE, d), lambda i: (i, 0)),
            pl.BlockSpec((1, d), const2),
            pl.BlockSpec((None, d, d_ff), pick, pipeline_mode=pl.Buffered(1)),
            pl.BlockSpec((None, d, d_ff), pick, pipeline_mode=pl.Buffered(1)),
            pl.BlockSpec((None, d_ff, d), pick, pipeline_mode=pl.Buffered(1)),
            pl.BlockSpec((1, d), const2),
        ],
        out_specs=pl.BlockSpec((FFN_TILE, d), lambda i: (i, 0)),
        out_shape=jax.ShapeDtypeStruct((n_tok, d), F32),
        compiler_params=pltpu.CompilerParams(
            dimension_semantics=("arbitrary",), vmem_limit_bytes=VMEM_LIMIT_BYTES),
        name="ffn",
    )(x2, norm_g.reshape(1, d), w_gate, w_up, w_down, final_g.reshape(1, d))


def _interleave(main, fill):
    n = len(main)
    for s, fn in enumerate(main):
        fn()
        for g in fill[s * len(fill) // n:(s + 1) * len(fill) // n]:
            g()


def _odd_layer_kernel(x_ref, g_ref, win_ref, poolw_ref, pscale_ref, dww_ref, dwb_ref, cng_ref,
                      cnb_ref, wout_ref, fg_ref, wg_ref, wu_ref, wd_ref, final_g_ref, o_ref,
                      ubuf, ybuf, yrot, cbuf, pool_out, xmid,
                      *, pool_width, tiles_per_seq, n_tiles, apply_final_norm):
    s = pl.program_id(0)
    ts = x_ref.shape[0]
    conv_width = ybuf.shape[1]
    d_ff = wg_ref.shape[1]
    seq_tile = s % tiles_per_seq

    @pl.when(seq_tile == 0)
    def _():
        ubuf[0:POOL_HALO, :] = jnp.zeros((POOL_HALO, ubuf.shape[1]), F32)
        ybuf[0:DW_HALO, :] = jnp.zeros((DW_HALO, ybuf.shape[1]), F32)

    @pl.when(seq_tile != 0)
    def _():
        ubuf[0:POOL_HALO, :] = ubuf[ts:ts + POOL_HALO, :]
        ybuf[0:DW_HALO, :] = ybuf[ts:ts + DW_HALO, :]

    @pl.when(s >= 1)
    def _():
        yf = _normalize(cbuf[...]) * cng_ref[...] + cnb_ref[...]
        xmid[...] = pool_out[...] + _dot(_silu(yf), wout_ref[pool_width:, :])

    pool_st = {}

    def pool(g, w):
        cols = slice(g * POOL_GROUP, (g + 1) * POOL_GROUP)
        pos = (seq_tile * ts + lax.broadcasted_iota(jnp.int32, (ts, 1), 0)).astype(F32) + 1.0
        wsum = ubuf[:, cols]
        shift = 1
        while shift < w:
            wsum = wsum + pltpu.roll(wsum, shift, axis=0)
            shift *= 2
        cur = ubuf[POOL_HALO:POOL_HALO + ts, cols]
        pooled = wsum[POOL_HALO:, :] / jnp.minimum(pos, float(w)) - cur
        y_pool = _dot(pooled, poolw_ref[g]) * pscale_ref[:, cols]
        prev = x_ref[...] if g == 0 else pool_st["out"]
        pool_st["out"] = prev + _dot(y_pool, wout_ref[cols, :])
        if g == len(POOL_WINDOWS) - 1:
            pool_out[...] = pool_st["out"]

    @pl.when(s < n_tiles)
    def _():
        h = _rmsnorm(x_ref[...], g_ref[...]).astype(BF16)
        z = jnp.dot(h, win_ref[...], preferred_element_type=F32)
        ubuf[POOL_HALO:POOL_HALO + ts, :] = z[:, :pool_width]
        ga = z[:, pool_width:pool_width + conv_width]
        gb = z[:, pool_width + conv_width:]
        ybuf[DW_HALO:DW_HALO + ts, :] = ga * _sigmoid(gb)
        for g, w in enumerate(POOL_WINDOWS):
            pool(g, w)

    rot_rows = yrot.shape[1]

    def shifted_copy(r):
        yrot[r - 1] = ybuf[r:r + rot_rows, :]

    def conv(r0, c0):
        cols = slice(c0, c0 + 128)
        acc = jnp.broadcast_to(dwb_ref[:, cols], (DW_ROWS, 128))
        for k in range(DW_KERNEL):
            start = DW_HALO - (DW_KERNEL - 1) + k + r0
            src = ybuf if start % 8 == 0 else yrot.at[start % 8 - 1]
            aligned = start - start % 8
            acc = acc + dww_ref[k:k + 1, cols] * src[aligned:aligned + DW_ROWS, cols]
        cbuf[r0:r0 + DW_ROWS, cols] = acc

    vector_pieces = [functools.partial(shifted_copy, r) for r in range(1, 8)]
    vector_pieces += [functools.partial(conv, r0, c0) for r0 in range(0, ts, DW_ROWS)
                      for c0 in range(0, conv_width, 128)]

    st = {}

    def ffn_start():
        st["acc"] = xmid[...]
        st["h"] = _rmsnorm(st["acc"], fg_ref[...]).astype(BF16)

    def ffn_gate(c0):
        st["gate"] = jnp.dot(st["h"], wg_ref[:, c0:c0 + FF_CHUNK], preferred_element_type=F32)

    def ffn_up(c0):
        up = jnp.dot(st["h"], wu_ref[:, c0:c0 + FF_CHUNK], preferred_element_type=F32)
        st["a"] = (_silu(st["gate"]) * up).astype(BF16)

    def ffn_down(c0):
        st["acc"] = st["acc"] + jnp.dot(st["a"], wd_ref[c0:c0 + FF_CHUNK, :],
                                        preferred_element_type=F32)

    def ffn_finish():
        acc = st["acc"]
        if apply_final_norm:
            acc = _rmsnorm(acc, final_g_ref[...])
        o_ref[...] = acc

    matmul_pieces = [ffn_start]
    for c0 in range(0, d_ff, FF_CHUNK):
        matmul_pieces += [functools.partial(f, c0) for f in (ffn_gate, ffn_up, ffn_down)]
    matmul_pieces += [ffn_finish]

    @pl.when(s == 0)
    def _():
        for piece in vector_pieces:
            piece()

    @pl.when(s >= 1)
    def _():
        _interleave(matmul_pieces, vector_pieces)


def _odd_layer(x2, batch, seq, norm_g, w_in, pool_w, pool_scale, dw_w, dw_b, cn_g, cn_b, w_out,
               ffn_norm_g, layer, w_gate, w_up, w_down, final_g, apply_final_norm):
    n_tok, d = x2.shape
    pool_width = pool_w.shape[0] * pool_w.shape[1]
    conv_width = dw_w.shape[1]
    d_ff = w_gate.shape[2]
    ts = SEQ_TILE
    nt = seq // ts
    n_tiles = batch * nt
    assert nt * ts == seq and ts % DW_ROWS == 0 and d_ff % FF_CHUNK == 0
    c2 = lambda s: (0, 0)
    c3 = lambda s: (0, 0, 0)
    pick = lambda s: (layer, 0, 0)
    resident = functools.partial(pl.BlockSpec, pipeline_mode=pl.Buffered(1))
    return pl.pallas_call(
        functools.partial(_odd_layer_kernel, pool_width=pool_width, tiles_per_seq=nt,
                          n_tiles=n_tiles, apply_final_norm=apply_final_norm),
        grid=(n_tiles + 1,),
        in_specs=[
            pl.BlockSpec((ts, d), lambda s: (jnp.minimum(s, n_tiles - 1), 0)),
            pl.BlockSpec((1, d), c2),
            resident(w_in.shape, c2),
            pl.BlockSpec(pool_w.shape, c3),
            pl.BlockSpec((1, pool_width), c2),
            pl.BlockSpec(dw_w.shape, c2),
            pl.BlockSpec((1, conv_width), c2),
            pl.BlockSpec((1, conv_width), c2),
            pl.BlockSpec((1, conv_width), c2),
            resident(w_out.shape, c2),
            pl.BlockSpec((1, d), c2),
            resident((None, d, d_ff), pick),
            resident((None, d, d_ff), pick),
            resident((None, d_ff, d), pick),
            pl.BlockSpec((1, d), c2),
        ],
        out_specs=pl.BlockSpec((ts, d), lambda s: (jnp.maximum(s - 1, 0), 0)),
        out_shape=jax.ShapeDtypeStruct((n_tok, d), F32),
        scratch_shapes=[
            pltpu.VMEM((ts + POOL_HALO, pool_width), F32),
            pltpu.VMEM((ts + DW_HALO, conv_width), F32),
            pltpu.VMEM((7, ts + DW_HALO - 8, conv_width), F32),
            pltpu.VMEM((ts, conv_width), F32),
            pltpu.VMEM((ts, d), F32),
            pltpu.VMEM((ts, d), F32),
        ],
        compiler_params=pltpu.CompilerParams(
            dimension_semantics=("arbitrary",), vmem_limit_bytes=VMEM_LIMIT_BYTES),
        name="odd_layer",
    )(x2, norm_g.reshape(1, d), w_in.astype(BF16), pool_w.astype(BF16),
      pool_scale.reshape(1, pool_width), dw_w, dw_b.reshape(1, conv_width),
      cn_g.reshape(1, conv_width), cn_b.reshape(1, conv_width), w_out.astype(BF16),
      ffn_norm_g.reshape(1, d), w_gate, w_up, w_down, final_g.reshape(1, d))


PAIR_WIDTH = 2 * HEAD_DIM
N_PAIRS = N_HEADS // 2
PAIR_COLS = 4 * PAIR_WIDTH


def _chunk_cumsum_rows(x, n):
    idx = lax.broadcasted_iota(jnp.int32, x.shape, 0) % n
    s = 1
    while s < n:
        x = x + jnp.where(idx >= s, pltpu.roll(x, s, axis=0), 0.0)
        s *= 2
    return x


def _chunk_cumsum_lanes(x, n):
    idx = lax.broadcasted_iota(jnp.int32, x.shape, 1) % n
    s = 1
    while s < n:
        x = x + jnp.where(idx >= s, pltpu.roll(x, s, axis=1), 0.0)
        s *= 2
    return x


def _halves(a2):
    return a2[:, :HEAD_DIM], a2[:, HEAD_DIM:]


def _pair(a0, a1):
    return jnp.concatenate([a0, a1], axis=1)


def _block_diag(a2):
    a0, a1 = _halves(a2)
    z = jnp.zeros_like(a0)
    return jnp.concatenate([_pair(a0, z), _pair(z, a1)], axis=0)


def _pair_head_norm(x2, g2):
    x0, x1 = _halves(x2)
    return _pair(_normalize(x0), _normalize(x1)) * g2


def _rotary_pair(t2, cos, sin):
    t0, t1 = _halves(t2)
    return _pair(t0 * cos + pltpu.roll(t0, HEAD_DIM // 2, axis=1) * sin,
                 t1 * cos + pltpu.roll(t1, HEAD_DIM // 2, axis=1) * sin)


def _even_kernel(x_ref, g_ref, win_ref, wgc_ref, wgr_ref, gbc_ref, gbr_ref, cw_ref, cb_ref,
                 cos_ref, sin_ref, mng_ref, rng_ref, wout_ref, o_ref,
                 zbuf, qkbuf, mixbuf, c_state, n_state, m_state, r_state, w_buf, *,
                 ret_log_decay):
    i = pl.program_id(1)
    ts = x_ref.shape[0]
    scale = HEAD_DIM ** -0.5
    n_chunks = ts // CHUNK
    width = N_HEADS * HEAD_DIM

    @pl.when((pl.program_id(0) == 0) & (i == 0))
    def _():
        ret_row0 = win_ref.shape[0] - 4 * width
        for blk in range(8 * width // PAIR_WIDTH):
            src = blk * PAIR_WIDTH
            if src >= 4 * width:
                src += ret_row0 - 4 * width
            w_buf[:, blk * PAIR_WIDTH:(blk + 1) * PAIR_WIDTH] = (
                win_ref[src:src + PAIR_WIDTH, :].T.astype(BF16))

    @pl.when(i == 0)
    def _():
        zbuf[0:QK_HALO, :] = jnp.zeros((QK_HALO, zbuf.shape[1]), F32)
        c_state[...] = jnp.zeros(c_state.shape, F32)
        n_state[...] = jnp.zeros(n_state.shape, F32)
        m_state[...] = jnp.full(m_state.shape, NEG, F32)
        r_state[...] = jnp.zeros(r_state.shape, F32)

    x = x_ref[...]
    h = _rmsnorm(x, g_ref[...]).astype(BF16)
    for pp in range(2 * N_PAIRS):
        for j in range(PAIR_COLS // PAIR_WIDTH):
            src = (pp // N_PAIRS) * 4 * width + j * width + (pp % N_PAIRS) * PAIR_WIDTH
            dst = pp * PAIR_COLS + j * PAIR_WIDTH
            zbuf[QK_HALO:QK_HALO + ts, dst:dst + PAIR_WIDTH] = jnp.dot(
                h, w_buf[:, src:src + PAIR_WIDTH], preferred_element_type=F32)

    gate_c = jnp.dot(h, wgc_ref[...].astype(BF16), preferred_element_type=F32) + gbc_ref[...]
    gate_r = _dot_nt(wgr_ref[...].astype(BF16), h) + gbr_ref[...]
    ig_c = gate_c[:, 0:N_HEADS]
    b_c = _chunk_cumsum_rows(_log_sigmoid(gate_c), CHUNK)[:, N_HEADS:2 * N_HEADS]
    ig_r = gate_r[0:N_HEADS, :]
    b_r = _chunk_cumsum_lanes(_log_sigmoid(gate_r), CHUNK)[N_HEADS:2 * N_HEADS, :]

    row_i = lax.broadcasted_iota(jnp.int32, (CHUNK, CHUNK), 0)
    col_j = lax.broadcasted_iota(jnp.int32, (CHUNK, CHUNK), 1)
    causal = col_j <= row_i
    rel = (row_i - col_j).astype(F32)
    pos_c = lax.broadcasted_iota(jnp.int32, (CHUNK, 1), 0).astype(F32)
    blk_r = lax.broadcasted_iota(jnp.int32, (PAIR_WIDTH, PAIR_WIDTH), 0) // HEAD_DIM
    blk_c = lax.broadcasted_iota(jnp.int32, (PAIR_WIDTH, PAIR_WIDTH), 1) // HEAD_DIM
    same_head = blk_r == blk_c
    ones_row = jnp.ones((1, HEAD_DIM), F32)

    for p in range(N_PAIRS):
        base = p * PAIR_COLS
        qk_cols = slice(p * 2 * PAIR_WIDTH, (p + 1) * 2 * PAIR_WIDTH)
        q_cols = slice(p * PAIR_WIDTH, (p + 1) * PAIR_WIDTH)
        k_cols = slice(width + p * PAIR_WIDTH, width + (p + 1) * PAIR_WIDTH)
        conv = jnp.broadcast_to(_pair(cb_ref[:, q_cols], cb_ref[:, k_cols]), (ts, 2 * PAIR_WIDTH))
        for k in range(QK_CONV):
            start = QK_HALO - (QK_CONV - 1) + k
            w_k = _pair(cw_ref[k:k + 1, q_cols], cw_ref[k:k + 1, k_cols])
            conv = conv + w_k * zbuf[start:start + ts, base:base + 2 * PAIR_WIDTH]
        qkbuf[:, qk_cols] = _silu(conv)

    def mlstm_a(p, c, st):
        base = p * PAIR_COLS
        rows = slice(c * CHUNK, (c + 1) * CHUNK)
        zrows = slice(QK_HALO + c * CHUNK, QK_HALO + (c + 1) * CHUNK)
        q2 = qkbuf[rows, p * 2 * PAIR_WIDTH: p * 2 * PAIR_WIDTH + PAIR_WIDTH]
        k2 = qkbuf[rows, p * 2 * PAIR_WIDTH + PAIR_WIDTH: (p + 1) * 2 * PAIR_WIDTH] * scale
        v2 = zbuf[zrows, base + 2 * PAIR_WIDTH: base + 3 * PAIR_WIDTH]
        k_h = _halves(k2)
        log_d, m, inter, a_end, s_old, s_new, m_new = [], [], [], [], [], [], []
        for e in range(2):
            hd = 2 * p + e
            m_prev = st["m"][e]
            bc = b_c[rows, hd:hd + 1]
            ic = ig_c[rows, hd:hd + 1]
            br = b_r[hd:hd + 1, rows]
            ir = ig_r[hd:hd + 1, rows]
            ld = jnp.where(causal, bc - br + ir, NEG)
            m_intra = jnp.max(ld, axis=1, keepdims=True)
            total = bc[CHUNK - 1:CHUNK, :]
            w_end = total - bc + ic
            m_loc = jnp.max(w_end, axis=0, keepdims=True)
            log_inter = bc + m_prev
            m_e = jnp.maximum(m_intra, log_inter)
            mn = jnp.maximum(total + m_prev, m_loc)
            log_d.append(ld - m_e)
            m.append(m_e)
            inter.append(jnp.exp(log_inter - m_e))
            a_end.append(jnp.exp(w_end - m_loc))
            s_old.append(jnp.exp(total + m_prev - mn))
            s_new.append(jnp.exp(m_loc - mn))
            m_new.append(mn)
        ak2 = _pair(a_end[0] * k_h[0], a_end[1] * k_h[1])
        return dict(rows=rows, zrows=zrows, q2=q2, v2=v2, m=m, inter=inter,
                    decay=jnp.exp(_pair(log_d[0], log_d[1])),
                    qk=_dot_nt(q2, _block_diag(k2)),
                    q_state=_dot(q2, st["c"]),
                    kv=_dot_tn(ak2, v2),
                    n_sum=jnp.sum(ak2, axis=0, keepdims=True),
                    s_old2=_pair(s_old[0] * ones_row, s_old[1] * ones_row),
                    s_new2=_pair(s_new[0] * ones_row, s_new[1] * ones_row),
                    m_new=m_new)

    def mlstm_c(p, st, t):
        base = p * PAIR_COLS
        og2 = zbuf[t["zrows"], base + 3 * PAIR_WIDTH: base + 4 * PAIR_WIDTH]
        s_h = _halves(t["s2"])
        q_h = _halves(t["q2"])
        n_h = _halves(st["n"])
        hm = []
        for e in range(2):
            num = _halves(t["sv"])[e] + t["inter"][e] * _halves(t["q_state"])[e]
            den = (jnp.sum(s_h[e], axis=1, keepdims=True)
                   + t["inter"][e] * jnp.sum(q_h[e] * n_h[e], axis=1, keepdims=True))
            hm.append(num / jnp.maximum(jnp.abs(den), jnp.exp(-t["m"][e])))
        g2 = _pair(mng_ref[2 * p:2 * p + 1, :], mng_ref[2 * p + 1:2 * p + 2, :])
        hm2 = _pair_head_norm(_sigmoid(og2) * _pair(hm[0], hm[1]), g2)
        mixbuf[t["rows"], p * PAIR_WIDTH:(p + 1) * PAIR_WIDTH] = hm2.astype(BF16)
        st["c"] = t["s_old2"] * st["c"] + t["s_new2"] * jnp.where(same_head, t["kv"], 0.0)
        st["n"] = t["s_old2"] * st["n"] + t["s_new2"] * t["n_sum"]
        st["m"] = t["m_new"]

    def ret_a(p, c, st):
        base = (N_PAIRS + p) * PAIR_COLS
        rows = slice(c * CHUNK, (c + 1) * CHUNK)
        zrows = slice(QK_HALO + c * CHUNK, QK_HALO + (c + 1) * CHUNK)
        cos = cos_ref[rows, :]
        sin = sin_ref[rows, :]
        q2 = _rotary_pair(zbuf[zrows, base:base + PAIR_WIDTH], cos, sin)
        k2 = _rotary_pair(zbuf[zrows, base + PAIR_WIDTH:base + 2 * PAIR_WIDTH], cos, sin) * scale
        v2 = zbuf[zrows, base + 2 * PAIR_WIDTH: base + 3 * PAIR_WIDTH]
        return dict(rows=rows, zrows=zrows, v2=v2, decay=st["decay2"],
                    qk=_dot_nt(q2, _block_diag(k2)),
                    q_state=_dot(q2, st["r"]),
                    kv=_dot_tn(st["w_end2"] * k2, v2))

    def ret_c(p, st, t):
        base = (N_PAIRS + p) * PAIR_COLS
        rg2 = zbuf[t["zrows"], base + 3 * PAIR_WIDTH: base + 4 * PAIR_WIDTH]
        o2 = t["sv"] + t["q_state"] * st["q_decay2"]
        g2 = _pair(rng_ref[2 * p:2 * p + 1, :], rng_ref[2 * p + 1:2 * p + 2, :])
        hr2 = _silu(rg2) * _pair_head_norm(o2, g2)
        mixbuf[t["rows"], (N_PAIRS + p) * PAIR_WIDTH:(N_PAIRS + p + 1) * PAIR_WIDTH] = hr2.astype(BF16)
        st["r"] = st["chunk_decay2"] * st["r"] + jnp.where(same_head, t["kv"], 0.0)

    def stage_b(t):
        t["s2"] = t["qk"] * t["decay"]
        t["sv"] = _dot(t["s2"], _block_diag(t["v2"]))

    lanes = []
    for p in range(N_PAIRS):
        lanes.append((functools.partial(mlstm_a, p), functools.partial(mlstm_c, p),
                      dict(c=c_state[p], n=n_state[p],
                           m=[m_state[2 * p + e][:, 0:1] for e in range(2)])))
    for p in range(N_PAIRS):
        decay, w_end_r, q_decay, chunk_decay = [], [], [], []
        for e in range(2):
            lg = ret_log_decay[2 * p + e]
            decay.append(jnp.where(causal, jnp.exp(lg * jnp.maximum(rel, 0.0)), 0.0))
            w_end_r.append(jnp.exp(lg * (CHUNK - 1.0 - pos_c)) * ones_row)
            q_decay.append(jnp.exp(lg * (pos_c + 1.0)) * ones_row)
            chunk_decay.append(math.exp(lg * CHUNK) * ones_row)
        lanes.append((functools.partial(ret_a, p), functools.partial(ret_c, p),
                      dict(r=r_state[p], decay2=_pair(*decay), w_end2=_pair(*w_end_r),
                           q_decay2=_pair(*q_decay), chunk_decay2=_pair(*chunk_decay))))

    for c in range(n_chunks):
        work = [stage_a(c, st) for stage_a, _, st in lanes]
        for t in work:
            stage_b(t)
        for (_, stage_c, st), t in zip(lanes, work):
            stage_c(st, t)

    for p in range(N_PAIRS):
        st = lanes[p][2]
        c_state[p] = st["c"]
        n_state[p] = st["n"]
        for e in range(2):
            m_state[2 * p + e] = st["m"][e] * ones_row
        r_state[p] = lanes[N_PAIRS + p][2]["r"]

    o_ref[...] = x + jnp.dot(mixbuf[...], wout_ref[...], preferred_element_type=F32)
    zbuf[0:QK_HALO, :] = zbuf[ts:ts + QK_HALO, :]


def _retention_log_decays():
    h = np.arange(N_HEADS, dtype=np.float32)
    lg = np.log(np.float32(1.0) - np.float32(2.0) ** (np.float32(-5.0) - h)).astype(np.float32)
    return tuple(float(v) for v in lg)


@functools.lru_cache(maxsize=None)
def _rotary_tables(seq):
    inv = np.float32(ROPE_BASE) ** (-np.arange(0, HEAD_DIM, 2, dtype=np.float32) / np.float32(HEAD_DIM))
    ang = np.arange(seq, dtype=np.float32)[:, None] * inv[None, :]
    cos = np.cos(ang).astype(np.float32)
    sin = np.sin(ang).astype(np.float32)
    return np.concatenate([cos, cos], axis=-1), np.concatenate([-sin, sin], axis=-1)


def _even_mixer(x2, batch, seq, norm_g, w_in_t, j, qk_conv_w, qk_conv_b, i_bias, f_bias,
                mlstm_norm_g, ret_norm_g, w_out):
    n_tok, d = x2.shape
    width = N_HEADS * HEAD_DIM
    ts = SEQ_TILE
    nt = seq // ts
    assert nt * ts == seq and ts % CHUNK == 0
    g0 = 4 * width
    g1 = g0 + 2 * N_HEADS
    assert w_in_t.shape[1:] == (g1 + 4 * width, d)
    w_gate_r = lax.slice(w_in_t, (j, g0, 0), (j + 1, g1, d)).reshape(g1 - g0, d)
    w_gate_c = jnp.pad(w_gate_r.T, ((0, 0), (0, HEAD_DIM - 2 * N_HEADS)))
    gate_b = jnp.concatenate([i_bias, f_bias]).astype(F32)
    gate_b_c = jnp.pad(gate_b, (0, HEAD_DIM - 2 * N_HEADS)).reshape(1, HEAD_DIM)
    gate_b_r = gate_b.reshape(2 * N_HEADS, 1)
    cos_t, sin_t = _rotary_tables(seq)

    row = lambda b, i: (b * nt + i, 0)
    seq_row = lambda b, i: (i, 0)
    c2 = lambda b, i: (0, 0)
    return pl.pallas_call(
        functools.partial(_even_kernel, ret_log_decay=_retention_log_decays()),
        grid=(batch, nt),
        in_specs=[
            pl.BlockSpec((ts, d), row),
            pl.BlockSpec((1, d), c2),
            pl.BlockSpec((None,) + w_in_t.shape[1:], lambda b, i: (j, 0, 0),
                         pipeline_mode=pl.Buffered(1)),
            pl.BlockSpec(w_gate_c.shape, c2),
            pl.BlockSpec(w_gate_r.shape, c2),
            pl.BlockSpec(gate_b_c.shape, c2),
            pl.BlockSpec(gate_b_r.shape, c2),
            pl.BlockSpec(qk_conv_w.shape, c2),
            pl.BlockSpec((1, 2 * width), c2),
            pl.BlockSpec((ts, HEAD_DIM), seq_row),
            pl.BlockSpec((ts, HEAD_DIM), seq_row),
            pl.BlockSpec(mlstm_norm_g.shape, c2),
            pl.BlockSpec(ret_norm_g.shape, c2),
            pl.BlockSpec(w_out.shape, c2),
        ],
        out_specs=pl.BlockSpec((ts, d), row),
        out_shape=jax.ShapeDtypeStruct((n_tok, d), F32),
        scratch_shapes=[
            pltpu.VMEM((ts + QK_HALO, 8 * width), F32),
            pltpu.VMEM((ts, 2 * width), F32),
            pltpu.VMEM((ts, 2 * width), BF16),
            pltpu.VMEM((N_PAIRS, PAIR_WIDTH, PAIR_WIDTH), F32),
            pltpu.VMEM((N_PAIRS, 1, PAIR_WIDTH), F32),
            pltpu.VMEM((N_HEADS, 1, HEAD_DIM), F32),
            pltpu.VMEM((N_PAIRS, PAIR_WIDTH, PAIR_WIDTH), F32),
            pltpu.VMEM((d, 8 * width), BF16),
        ],
        compiler_params=pltpu.CompilerParams(
            dimension_semantics=("arbitrary", "arbitrary"), vmem_limit_bytes=VMEM_LIMIT_BYTES),
        name="even_mixer",
    )(x2, norm_g.reshape(1, d), w_in_t, w_gate_c, w_gate_r, gate_b_c, gate_b_r,
      qk_conv_w, qk_conv_b.reshape(1, 2 * width), jnp.asarray(cos_t), jnp.asarray(sin_t),
      mlstm_norm_g, ret_norm_g, w_out.astype(BF16))


def kernel(x, norm_mix_g, norm_ffn_g, final_norm_g, ev_w_in, ev_qk_conv_w, ev_qk_conv_b, ev_i_bias, ev_f_bias, ev_mlstm_norm_g, ev_ret_norm_g, ev_w_out, od_w_in, od_pool_w, od_pool_scale, od_dw_w, od_dw_b, od_conv_norm_g, od_conv_norm_b, od_w_out, ffn_w_gate, ffn_w_up, ffn_w_down):
    batch, seq, d = x.shape
    depth = norm_mix_g.shape[0]
    x2 = x.reshape(batch * seq, d)
    w_gate, w_up, w_down = (w.astype(BF16) for w in (ffn_w_gate, ffn_w_up, ffn_w_down))
    ev_w_in_t = jnp.swapaxes(ev_w_in, 1, 2)
    for layer in range(depth):
        j = layer // 2
        last = layer == depth - 1
        if layer % 2 == 0:
            x2 = _even_mixer(x2, batch, seq, norm_mix_g[layer], ev_w_in_t, j,
                             ev_qk_conv_w[j], ev_qk_conv_b[j], ev_i_bias[j], ev_f_bias[j],
                             ev_mlstm_norm_g[j], ev_ret_norm_g[j], ev_w_out[j])
            x2 = _ffn(x2, layer, norm_ffn_g[layer], w_gate, w_up, w_down, final_norm_g,
                      apply_final_norm=last)
        else:
            x2 = _odd_layer(x2, batch, seq, norm_mix_g[layer], od_w_in[j], od_pool_w[j],
                            od_pool_scale[j], od_dw_w[j], od_dw_b[j], od_conv_norm_g[j],
                            od_conv_norm_b[j], od_w_out[j], norm_ffn_g[layer], layer, w_gate,
                            w_up, w_down, final_norm_g, apply_final_norm=last)
    return x2.reshape(batch, seq, d)
```

```python
import functools
import math

import numpy as np
import jax
import jax.numpy as jnp
from jax import lax
from jax.experimental import pallas as pl
from jax.experimental.pallas import tpu as pltpu

F32 = jnp.float32
BF16 = jnp.bfloat16

EPS = 1e-6
NEG = -1e30
CHUNK = 128
HEAD_DIM = 128
N_HEADS = 4
QK_CONV = 4
ROPE_BASE = 10000.0
POOL_WINDOWS = (2, 4, 8, 16)
POOL_GROUP = 128
DW_KERNEL = 31

SEQ_TILE = 512
FFN_TILE = 1024
FF_CHUNK = 256
QK_HALO = 8
POOL_HALO = 16
DW_HALO = 32
DW_ROWS = 64
VMEM_LIMIT_BYTES = 56 * 1024 * 1024


def _rmsnorm(x, g):
    ms = jnp.mean(x * x, axis=-1, keepdims=True)
    return (x * lax.rsqrt(ms + EPS)) * g


def _normalize(x):
    xc = x - jnp.mean(x, axis=-1, keepdims=True)
    return xc * lax.rsqrt(jnp.mean(xc * xc, axis=-1, keepdims=True) + EPS)


def _sigmoid(x):
    return 1.0 / (1.0 + jnp.exp(-x))


def _silu(x):
    return x * _sigmoid(x)


def _log_sigmoid(x):
    return jnp.minimum(x, 0.0) - jnp.log(1.0 + jnp.exp(-jnp.abs(x)))


def _dot(a, b):
    return jnp.dot(a.astype(BF16), b.astype(BF16), preferred_element_type=F32)


def _dot_nt(a, b):
    return lax.dot_general(a.astype(BF16), b.astype(BF16), (((1,), (1,)), ((), ())),
                           preferred_element_type=F32)


def _dot_tn(a, b):
    return lax.dot_general(a.astype(BF16), b.astype(BF16), (((0,), (0,)), ((), ())),
                           preferred_element_type=F32)


W_STAGE_CHUNKS = 16


def _load_ffn_weights(layer, sources, dests, stages, sem):
    chunks = []
    for src, dst, stage in zip(sources, dests, stages):
        rows = stage.shape[1]
        for r0 in range(0, dst.shape[0], rows):
            chunks.append((src.at[layer, pl.ds(r0, rows)], dst, r0, rows, stage))

    def dma(n):
        return pltpu.make_async_copy(chunks[n][0], chunks[n][4].at[n % 2], sem.at[n % 2])

    dma(0).start()
    for n, (_, dst, r0, rows, stage) in enumerate(chunks):
        if n + 1 < len(chunks):
            dma(n + 1).start()
        dma(n).wait()
        dst[r0:r0 + rows, :] = stage[n % 2].astype(BF16)


def _ffn_weight_scratch(d, d_ff):
    assert d % (16 * W_STAGE_CHUNKS) == 0 and d_ff % (16 * W_STAGE_CHUNKS) == 0
    return [
        pltpu.VMEM((d, d_ff), BF16),
        pltpu.VMEM((d, d_ff), BF16),
        pltpu.VMEM((d_ff, d), BF16),
        pltpu.VMEM((2, d // W_STAGE_CHUNKS, d_ff), F32),
        pltpu.VMEM((2, d_ff // W_STAGE_CHUNKS, d), F32),
        pltpu.SemaphoreType.DMA((2,)),
    ]


def _ffn_kernel(x_ref, g_ref, wg_hbm, wu_hbm, wd_hbm, fg_ref, o_ref,
                wg_ref, wu_ref, wd_ref, stage_gu, stage_d, wsem, *, layer, apply_final_norm):
    @pl.when(pl.program_id(0) == 0)
    def _():
        _load_ffn_weights(layer, (wg_hbm, wu_hbm, wd_hbm), (wg_ref, wu_ref, wd_ref),
                          (stage_gu, stage_gu, stage_d), wsem)

    x = x_ref[...]
    h = _rmsnorm(x, g_ref[...]).astype(BF16)
    acc = x
    d_ff = wg_ref.shape[1]
    for c0 in range(0, d_ff, FF_CHUNK):
        cols = slice(c0, c0 + FF_CHUNK)
        gate = jnp.dot(h, wg_ref[:, cols], preferred_element_type=F32)
        up = jnp.dot(h, wu_ref[:, cols], preferred_element_type=F32)
        a = (_silu(gate) * up).astype(BF16)
        acc = acc + jnp.dot(a, wd_ref[cols, :], preferred_element_type=F32)
    if apply_final_norm:
        acc = _rmsnorm(acc, fg_ref[...])
    o_ref[...] = acc


def _ffn(x2, layer, norm_g, w_gate, w_up, w_down, final_g, apply_final_norm):
    n_tok, d = x2.shape
    d_ff = w_gate.shape[2]
    assert d_ff % FF_CHUNK == 0 and n_tok % FFN_TILE == 0
    const2 = lambda i: (0, 0)
    in_hbm = pl.BlockSpec(memory_space=pl.ANY)
    return pl.pallas_call(
        functools.partial(_ffn_kernel, layer=layer, apply_final_norm=apply_final_norm),
        grid=(n_tok // FFN_TILE,),
        in_specs=[
            pl.BlockSpec((FFN_TILE, d), lambda i: (i, 0)),
            pl.BlockSpec((1, d), const2),
            in_hbm, in_hbm, in_hbm,
            pl.BlockSpec((1, d), const2),
        ],
        out_specs=pl.BlockSpec((FFN_TILE, d), lambda i: (i, 0)),
        out_shape=jax.ShapeDtypeStruct((n_tok, d), F32),
        scratch_shapes=_ffn_weight_scratch(d, d_ff),
        compiler_params=pltpu.CompilerParams(
            dimension_semantics=("arbitrary",), vmem_limit_bytes=VMEM_LIMIT_BYTES),
        name="ffn",
    )(x2, norm_g.reshape(1, d), w_gate, w_up, w_down, final_g.reshape(1, d))


def _interleave(main, fill):
    n = len(main)
    for s, fn in enumerate(main):
        fn()
        for g in fill[s * len(fill) // n:(s + 1) * len(fill) // n]:
            g()


def _odd_layer_kernel(x_ref, g_ref, win_ref, poolw_ref, pscale_ref, dww_ref, dwb_ref, cng_ref,
                      cnb_ref, wout_ref, fg_ref, wg_hbm, wu_hbm, wd_hbm, final_g_ref, o_ref,
                      ubuf, ybuf, yrot, cbuf, pool_out, xmid,
                      wg_ref, wu_ref, wd_ref, stage_gu, stage_d, wsem,
                      *, pool_width, tiles_per_seq, n_tiles, layer, apply_final_norm):
    s = pl.program_id(0)
    ts = x_ref.shape[0]
    conv_width = ybuf.shape[1]
    d_ff = wg_ref.shape[1]
    seq_tile = s % tiles_per_seq

    @pl.when(s == 0)
    def _():
        _load_ffn_weights(layer, (wg_hbm, wu_hbm, wd_hbm), (wg_ref, wu_ref, wd_ref),
                          (stage_gu, stage_gu, stage_d), wsem)

    @pl.when(seq_tile == 0)
    def _():
        ubuf[0:POOL_HALO, :] = jnp.zeros((POOL_HALO, ubuf.shape[1]), F32)
        ybuf[0:DW_HALO, :] = jnp.zeros((DW_HALO, ybuf.shape[1]), F32)

    @pl.when(seq_tile != 0)
    def _():
        ubuf[0:POOL_HALO, :] = ubuf[ts:ts + POOL_HALO, :]
        ybuf[0:DW_HALO, :] = ybuf[ts:ts + DW_HALO, :]

    @pl.when(s >= 1)
    def _():
        yf = _normalize(cbuf[...]) * cng_ref[...] + cnb_ref[...]
        xmid[...] = pool_out[...] + _dot(_silu(yf), wout_ref[pool_width:, :])

    pool_st = {}

    def pool(g, w):
        cols = slice(g * POOL_GROUP, (g + 1) * POOL_GROUP)
        pos = (seq_tile * ts + lax.broadcasted_iota(jnp.int32, (ts, 1), 0)).astype(F32) + 1.0
        wsum = ubuf[:, cols]
        shift = 1
        while shift < w:
            wsum = wsum + pltpu.roll(wsum, shift, axis=0)
            shift *= 2
        cur = ubuf[POOL_HALO:POOL_HALO + ts, cols]
        pooled = wsum[POOL_HALO:, :] / jnp.minimum(pos, float(w)) - cur
        y_pool = _dot(pooled, poolw_ref[g]) * pscale_ref[:, cols]
        prev = x_ref[...] if g == 0 else pool_st["out"]
        pool_st["out"] = prev + _dot(y_pool, wout_ref[cols, :])
        if g == len(POOL_WINDOWS) - 1:
            pool_out[...] = pool_st["out"]

    @pl.when(s < n_tiles)
    def _():
        h = _rmsnorm(x_ref[...], g_ref[...]).astype(BF16)
        z = jnp.dot(h, win_ref[...], preferred_element_type=F32)
        ubuf[POOL_HALO:POOL_HALO + ts, :] = z[:, :pool_width]
        ga = z[:, pool_width:pool_width + conv_width]
        gb = z[:, pool_width + conv_width:]
        ybuf[DW_HALO:DW_HALO + ts, :] = ga * _sigmoid(gb)
        for g, w in enumerate(POOL_WINDOWS):
            pool(g, w)

    rot_rows = yrot.shape[1]

    def shifted_copy(r):
        yrot[r - 1] = ybuf[r:r + rot_rows, :]

    def conv(r0, c0):
        cols = slice(c0, c0 + 128)
        acc = jnp.broadcast_to(dwb_ref[:, cols], (DW_ROWS, 128))
        for k in range(DW_KERNEL):
            start = DW_HALO - (DW_KERNEL - 1) + k + r0
            src = ybuf if start % 8 == 0 else yrot.at[start % 8 - 1]
            aligned = start - start % 8
            acc = acc + dww_ref[k:k + 1, cols] * src[aligned:aligned + DW_ROWS, cols]
        cbuf[r0:r0 + DW_ROWS, cols] = acc

    vector_pieces = [functools.partial(shifted_copy, r) for r in range(1, 8)]
    vector_pieces += [functools.partial(conv, r0, c0) for r0 in range(0, ts, DW_ROWS)
                      for c0 in range(0, conv_width, 128)]

    st = {}

    def ffn_start():
        st["acc"] = xmid[...]
        st["h"] = _rmsnorm(st["acc"], fg_ref[...]).astype(BF16)

    def ffn_gate(c0):
        st["gate"] = jnp.dot(st["h"], wg_ref[:, c0:c0 + FF_CHUNK], preferred_element_type=F32)

    def ffn_up(c0):
        up = jnp.dot(st["h"], wu_ref[:, c0:c0 + FF_CHUNK], preferred_element_type=F32)
        st["a"] = (_silu(st["gate"]) * up).astype(BF16)

    def ffn_down(c0):
        st["acc"] = st["acc"] + jnp.dot(st["a"], wd_ref[c0:c0 + FF_CHUNK, :],
                                        preferred_element_type=F32)

    def ffn_finish():
        acc = st["acc"]
        if apply_final_norm:
            acc = _rmsnorm(acc, final_g_ref[...])
        o_ref[...] = acc

    matmul_pieces = [ffn_start]
    for c0 in range(0, d_ff, FF_CHUNK):
        matmul_pieces += [functools.partial(f, c0) for f in (ffn_gate, ffn_up, ffn_down)]
    matmul_pieces += [ffn_finish]

    @pl.when(s == 0)
    def _():
        for piece in vector_pieces:
            piece()

    @pl.when(s >= 1)
    def _():
        _interleave(matmul_pieces, vector_pieces)


def _odd_layer(x2, batch, seq, norm_g, w_in, pool_w, pool_scale, dw_w, dw_b, cn_g, cn_b, w_out,
               ffn_norm_g, layer, w_gate, w_up, w_down, final_g, apply_final_norm):
    n_tok, d = x2.shape
    pool_width = pool_w.shape[0] * pool_w.shape[1]
    conv_width = dw_w.shape[1]
    d_ff = w_gate.shape[2]
    ts = SEQ_TILE
    nt = seq // ts
    n_tiles = batch * nt
    assert nt * ts == seq and ts % DW_ROWS == 0 and d_ff % FF_CHUNK == 0
    c2 = lambda s: (0, 0)
    c3 = lambda s: (0, 0, 0)
    in_hbm = pl.BlockSpec(memory_space=pl.ANY)
    resident = functools.partial(pl.BlockSpec, pipeline_mode=pl.Buffered(1))
    return pl.pallas_call(
        functools.partial(_odd_layer_kernel, pool_width=pool_width, tiles_per_seq=nt,
                          n_tiles=n_tiles, layer=layer, apply_final_norm=apply_final_norm),
        grid=(n_tiles + 1,),
        in_specs=[
            pl.BlockSpec((ts, d), lambda s: (jnp.minimum(s, n_tiles - 1), 0)),
            pl.BlockSpec((1, d), c2),
            resident(w_in.shape, c2),
            pl.BlockSpec(pool_w.shape, c3),
            pl.BlockSpec((1, pool_width), c2),
            pl.BlockSpec(dw_w.shape, c2),
            pl.BlockSpec((1, conv_width), c2),
            pl.BlockSpec((1, conv_width), c2),
            pl.BlockSpec((1, conv_width), c2),
            resident(w_out.shape, c2),
            pl.BlockSpec((1, d), c2),
            in_hbm, in_hbm, in_hbm,
            pl.BlockSpec((1, d), c2),
        ],
        out_specs=pl.BlockSpec((ts, d), lambda s: (jnp.maximum(s - 1, 0), 0)),
        out_shape=jax.ShapeDtypeStruct((n_tok, d), F32),
        scratch_shapes=[
            pltpu.VMEM((ts + POOL_HALO, pool_width), F32),
            pltpu.VMEM((ts + DW_HALO, conv_width), F32),
            pltpu.VMEM((7, ts + DW_HALO - 8, conv_width), F32),
            pltpu.VMEM((ts, conv_width), F32),
            pltpu.VMEM((ts, d), F32),
            pltpu.VMEM((ts, d), F32),
        ] + _ffn_weight_scratch(d, d_ff),
        compiler_params=pltpu.CompilerParams(
            dimension_semantics=("arbitrary",), vmem_limit_bytes=VMEM_LIMIT_BYTES),
        name="odd_layer",
    )(x2, norm_g.reshape(1, d), w_in.astype(BF16), pool_w.astype(BF16),
      pool_scale.reshape(1, pool_width), dw_w, dw_b.reshape(1, conv_width),
      cn_g.reshape(1, conv_width), cn_b.reshape(1, conv_width), w_out.astype(BF16),
      ffn_norm_g.reshape(1, d), w_gate, w_up, w_down, final_g.reshape(1, d))


PAIR_WIDTH = 2 * HEAD_DIM
N_PAIRS = N_HEADS // 2
PAIR_COLS = 4 * PAIR_WIDTH


def _chunk_cumsum_rows(x, n):
    idx = lax.broadcasted_iota(jnp.int32, x.shape, 0) % n
    s = 1
    while s < n:
        x = x + jnp.where(idx >= s, pltpu.roll(x, s, axis=0), 0.0)
        s *= 2
    return x


def _chunk_cumsum_lanes(x, n):
    idx = lax.broadcasted_iota(jnp.int32, x.shape, 1) % n
    s = 1
    while s < n:
        x = x + jnp.where(idx >= s, pltpu.roll(x, s, axis=1), 0.0)
        s *= 2
    return x


def _halves(a2):
    return a2[:, :HEAD_DIM], a2[:, HEAD_DIM:]


def _pair(a0, a1):
    return jnp.concatenate([a0, a1], axis=1)


def _block_diag(a2):
    a0, a1 = _halves(a2)
    z = jnp.zeros_like(a0)
    return jnp.concatenate([_pair(a0, z), _pair(z, a1)], axis=0)


def _pair_head_norm(x2, g2):
    x0, x1 = _halves(x2)
    return _pair(_normalize(x0), _normalize(x1)) * g2


def _rotary_pair(t2, cos, sin):
    t0, t1 = _halves(t2)
    return _pair(t0 * cos + pltpu.roll(t0, HEAD_DIM // 2, axis=1) * sin,
                 t1 * cos + pltpu.roll(t1, HEAD_DIM // 2, axis=1) * sin)


def _even_kernel(x_ref, g_ref, win_ref, wgc_ref, wgr_ref, gbc_ref, gbr_ref, cw_ref, cb_ref,
                 cos_ref, sin_ref, mng_ref, rng_ref, wout_ref, o_ref,
                 zbuf, qkbuf, mixbuf, c_state, n_state, m_state, r_state, w_buf, *,
                 ret_log_decay):
    i = pl.program_id(1)
    ts = x_ref.shape[0]
    scale = HEAD_DIM ** -0.5
    n_chunks = ts // CHUNK
    width = N_HEADS * HEAD_DIM

    @pl.when((pl.program_id(0) == 0) & (i == 0))
    def _():
        ret_row0 = win_ref.shape[0] - 4 * width
        for blk in range(8 * width // PAIR_WIDTH):
            src = blk * PAIR_WIDTH
            if src >= 4 * width:
                src += ret_row0 - 4 * width
            w_buf[:, blk * PAIR_WIDTH:(blk + 1) * PAIR_WIDTH] = (
                win_ref[src:src + PAIR_WIDTH, :].T.astype(BF16))

    @pl.when(i == 0)
    def _():
        zbuf[0:QK_HALO, :] = jnp.zeros((QK_HALO, zbuf.shape[1]), F32)
        c_state[...] = jnp.zeros(c_state.shape, F32)
        n_state[...] = jnp.zeros(n_state.shape, F32)
        m_state[...] = jnp.full(m_state.shape, NEG, F32)
        r_state[...] = jnp.zeros(r_state.shape, F32)

    x = x_ref[...]
    h = _rmsnorm(x, g_ref[...]).astype(BF16)
    for pp in range(2 * N_PAIRS):
        for j in range(PAIR_COLS // PAIR_WIDTH):
            src = (pp // N_PAIRS) * 4 * width + j * width + (pp % N_PAIRS) * PAIR_WIDTH
            dst = pp * PAIR_COLS + j * PAIR_WIDTH
            zbuf[QK_HALO:QK_HALO + ts, dst:dst + PAIR_WIDTH] = jnp.dot(
                h, w_buf[:, src:src + PAIR_WIDTH], preferred_element_type=F32)

    gate_c = jnp.dot(h, wgc_ref[...].astype(BF16), preferred_element_type=F32) + gbc_ref[...]
    gate_r = _dot_nt(wgr_ref[...].astype(BF16), h) + gbr_ref[...]
    ig_c = gate_c[:, 0:N_HEADS]
    b_c = _chunk_cumsum_rows(_log_sigmoid(gate_c), CHUNK)[:, N_HEADS:2 * N_HEADS]
    ig_r = gate_r[0:N_HEADS, :]
    b_r = _chunk_cumsum_lanes(_log_sigmoid(gate_r), CHUNK)[N_HEADS:2 * N_HEADS, :]

    row_i = lax.broadcasted_iota(jnp.int32, (CHUNK, CHUNK), 0)
    col_j = lax.broadcasted_iota(jnp.int32, (CHUNK, CHUNK), 1)
    causal = col_j <= row_i
    rel = (row_i - col_j).astype(F32)
    pos_c = lax.broadcasted_iota(jnp.int32, (CHUNK, 1), 0).astype(F32)
    blk_r = lax.broadcasted_iota(jnp.int32, (PAIR_WIDTH, PAIR_WIDTH), 0) // HEAD_DIM
    blk_c = lax.broadcasted_iota(jnp.int32, (PAIR_WIDTH, PAIR_WIDTH), 1) // HEAD_DIM
    same_head = blk_r == blk_c
    ones_row = jnp.ones((1, HEAD_DIM), F32)

    for p in range(N_PAIRS):
        base = p * PAIR_COLS
        qk_cols = slice(p * 2 * PAIR_WIDTH, (p + 1) * 2 * PAIR_WIDTH)
        q_cols = slice(p * PAIR_WIDTH, (p + 1) * PAIR_WIDTH)
        k_cols = slice(width + p * PAIR_WIDTH, width + (p + 1) * PAIR_WIDTH)
        conv = jnp.broadcast_to(_pair(cb_ref[:, q_cols], cb_ref[:, k_cols]), (ts, 2 * PAIR_WIDTH))
        for k in range(QK_CONV):
            start = QK_HALO - (QK_CONV - 1) + k
            w_k = _pair(cw_ref[k:k + 1, q_cols], cw_ref[k:k + 1, k_cols])
            conv = conv + w_k * zbuf[start:start + ts, base:base + 2 * PAIR_WIDTH]
        qkbuf[:, qk_cols] = _silu(conv)

    def mlstm_a(p, c, st):
        base = p * PAIR_COLS
        rows = slice(c * CHUNK, (c + 1) * CHUNK)
        zrows = slice(QK_HALO + c * CHUNK, QK_HALO + (c + 1) * CHUNK)
        q2 = qkbuf[rows, p * 2 * PAIR_WIDTH: p * 2 * PAIR_WIDTH + PAIR_WIDTH]
        k2 = qkbuf[rows, p * 2 * PAIR_WIDTH + PAIR_WIDTH: (p + 1) * 2 * PAIR_WIDTH] * scale
        v2 = zbuf[zrows, base + 2 * PAIR_WIDTH: base + 3 * PAIR_WIDTH]
        k_h = _halves(k2)
        log_d, m, inter, a_end, s_old, s_new, m_new = [], [], [], [], [], [], []
        for e in range(2):
            hd = 2 * p + e
            m_prev = st["m"][e]
            bc = b_c[rows, hd:hd + 1]
            ic = ig_c[rows, hd:hd + 1]
            br = b_r[hd:hd + 1, rows]
            ir = ig_r[hd:hd + 1, rows]
            ld = jnp.where(causal, bc - br + ir, NEG)
            m_intra = jnp.max(ld, axis=1, keepdims=True)
            total = bc[CHUNK - 1:CHUNK, :]
            w_end = total - bc + ic
            m_loc = jnp.max(w_end, axis=0, keepdims=True)
            log_inter = bc + m_prev
            m_e = jnp.maximum(m_intra, log_inter)
            mn = jnp.maximum(total + m_prev, m_loc)
            log_d.append(ld - m_e)
            m.append(m_e)
            inter.append(jnp.exp(log_inter - m_e))
            a_end.append(jnp.exp(w_end - m_loc))
            s_old.append(jnp.exp(total + m_prev - mn))
            s_new.append(jnp.exp(m_loc - mn))
            m_new.append(mn)
        ak2 = _pair(a_end[0] * k_h[0], a_end[1] * k_h[1])
        return dict(rows=rows, zrows=zrows, q2=q2, v2=v2, m=m, inter=inter,
                    decay=jnp.exp(_pair(log_d[0], log_d[1])),
                    qk=_dot_nt(q2, _block_diag(k2)),
                    q_state=_dot(q2, st["c"]),
                    kv=_dot_tn(ak2, v2),
                    n_sum=jnp.sum(ak2, axis=0, keepdims=True),
                    s_old2=_pair(s_old[0] * ones_row, s_old[1] * ones_row),
                    s_new2=_pair(s_new[0] * ones_row, s_new[1] * ones_row),
                    m_new=m_new)

    def mlstm_c(p, st, t):
        base = p * PAIR_COLS
        og2 = zbuf[t["zrows"], base + 3 * PAIR_WIDTH: base + 4 * PAIR_WIDTH]
        s_h = _halves(t["s2"])
        q_h = _halves(t["q2"])
        n_h = _halves(st["n"])
        hm = []
        for e in range(2):
            num = _halves(t["sv"])[e] + t["inter"][e] * _halves(t["q_state"])[e]
            den = (jnp.sum(s_h[e], axis=1, keepdims=True)
                   + t["inter"][e] * jnp.sum(q_h[e] * n_h[e], axis=1, keepdims=True))
            hm.append(num / jnp.maximum(jnp.abs(den), jnp.exp(-t["m"][e])))
        g2 = _pair(mng_ref[2 * p:2 * p + 1, :], mng_ref[2 * p + 1:2 * p + 2, :])
        hm2 = _pair_head_norm(_sigmoid(og2) * _pair(hm[0], hm[1]), g2)
        mixbuf[t["rows"], p * PAIR_WIDTH:(p + 1) * PAIR_WIDTH] = hm2.astype(BF16)
        st["c"] = t["s_old2"] * st["c"] + t["s_new2"] * jnp.where(same_head, t["kv"], 0.0)
        st["n"] = t["s_old2"] * st["n"] + t["s_new2"] * t["n_sum"]
        st["m"] = t["m_new"]

    def ret_a(p, c, st):
        base = (N_PAIRS + p) * PAIR_COLS
        rows = slice(c * CHUNK, (c + 1) * CHUNK)
        zrows = slice(QK_HALO + c * CHUNK, QK_HALO + (c + 1) * CHUNK)
        cos = cos_ref[rows, :]
        sin = sin_ref[rows, :]
        q2 = _rotary_pair(zbuf[zrows, base:base + PAIR_WIDTH], cos, sin)
        k2 = _rotary_pair(zbuf[zrows, base + PAIR_WIDTH:base + 2 * PAIR_WIDTH], cos, sin) * scale
        v2 = zbuf[zrows, base + 2 * PAIR_WIDTH: base + 3 * PAIR_WIDTH]
        return dict(rows=rows, zrows=zrows, v2=v2, decay=st["decay2"],
                    qk=_dot_nt(q2, _block_diag(k2)),
                    q_state=_dot(q2, st["r"]),
                    kv=_dot_tn(st["w_end2"] * k2, v2))

    def ret_c(p, st, t):
        base = (N_PAIRS + p) * PAIR_COLS
        rg2 = zbuf[t["zrows"], base + 3 * PAIR_WIDTH: base + 4 * PAIR_WIDTH]
        o2 = t["sv"] + t["q_state"] * st["q_decay2"]
        g2 = _pair(rng_ref[2 * p:2 * p + 1, :], rng_ref[2 * p + 1:2 * p + 2, :])
        hr2 = _silu(rg2) * _pair_head_norm(o2, g2)
        mixbuf[t["rows"], (N_PAIRS + p) * PAIR_WIDTH:(N_PAIRS + p + 1) * PAIR_WIDTH] = hr2.astype(BF16)
        st["r"] = st["chunk_decay2"] * st["r"] + jnp.where(same_head, t["kv"], 0.0)

    def stage_b(t):
        t["s2"] = t["qk"] * t["decay"]
        t["sv"] = _dot(t["s2"], _block_diag(t["v2"]))

    lanes = []
    for p in range(N_PAIRS):
        lanes.append((functools.partial(mlstm_a, p), functools.partial(mlstm_c, p),
                      dict(c=c_state[p], n=n_state[p],
                           m=[m_state[2 * p + e][:, 0:1] for e in range(2)])))
    for p in range(N_PAIRS):
        decay, w_end_r, q_decay, chunk_decay = [], [], [], []
        for e in range(2):
            lg = ret_log_decay[2 * p + e]
            decay.append(jnp.where(causal, jnp.exp(lg * jnp.maximum(rel, 0.0)), 0.0))
            w_end_r.append(jnp.exp(lg * (CHUNK - 1.0 - pos_c)) * ones_row)
            q_decay.append(jnp.exp(lg * (pos_c + 1.0)) * ones_row)
            chunk_decay.append(math.exp(lg * CHUNK) * ones_row)
        lanes.append((functools.partial(ret_a, p), functools.partial(ret_c, p),
                      dict(r=r_state[p], decay2=_pair(*decay), w_end2=_pair(*w_end_r),
                           q_decay2=_pair(*q_decay), chunk_decay2=_pair(*chunk_decay))))

    for c in range(n_chunks):
        work = [stage_a(c, st) for stage_a, _, st in lanes]
        for t in work:
            stage_b(t)
        for (_, stage_c, st), t in zip(lanes, work):
            stage_c(st, t)

    for p in range(N_PAIRS):
        st = lanes[p][2]
        c_state[p] = st["c"]
        n_state[p] = st["n"]
        for e in range(2):
            m_state[2 * p + e] = st["m"][e] * ones_row
        r_state[p] = lanes[N_PAIRS + p][2]["r"]

    o_ref[...] = x + jnp.dot(mixbuf[...], wout_ref[...], preferred_element_type=F32)
    zbuf[0:QK_HALO, :] = zbuf[ts:ts + QK_HALO, :]


def _retention_log_decays():
    h = np.arange(N_HEADS, dtype=np.float32)
    lg = np.log(np.float32(1.0) - np.float32(2.0) ** (np.float32(-5.0) - h)).astype(np.float32)
    return tuple(float(v) for v in lg)


@functools.lru_cache(maxsize=None)
def _rotary_tables(seq):
    inv = np.float32(ROPE_BASE) ** (-np.arange(0, HEAD_DIM, 2, dtype=np.float32) / np.float32(HEAD_DIM))
    ang = np.arange(seq, dtype=np.float32)[:, None] * inv[None, :]
    cos = np.cos(ang).astype(np.float32)
    sin = np.sin(ang).astype(np.float32)
    return np.concatenate([cos, cos], axis=-1), np.concatenate([-sin, sin], axis=-1)


def _even_mixer(x2, batch, seq, norm_g, w_in_t, j, qk_conv_w, qk_conv_b, i_bias, f_bias,
                mlstm_norm_g, ret_norm_g, w_out):
    n_tok, d = x2.shape
    width = N_HEADS * HEAD_DIM
    ts = SEQ_TILE
    nt = seq // ts
    assert nt * ts == seq and ts % CHUNK == 0
    g0 = 4 * width
    g1 = g0 + 2 * N_HEADS
    assert w_in_t.shape[1:] == (g1 + 4 * width, d)
    w_gate_r = lax.slice(w_in_t, (j, g0, 0), (j + 1, g1, d)).reshape(g1 - g0, d)
    w_gate_c = jnp.pad(w_gate_r.T, ((0, 0), (0, HEAD_DIM - 2 * N_HEADS)))
    gate_b = jnp.concatenate([i_bias, f_bias]).astype(F32)
    gate_b_c = jnp.pad(gate_b, (0, HEAD_DIM - 2 * N_HEADS)).reshape(1, HEAD_DIM)
    gate_b_r = gate_b.reshape(2 * N_HEADS, 1)
    cos_t, sin_t = _rotary_tables(seq)

    row = lambda b, i: (b * nt + i, 0)
    seq_row = lambda b, i: (i, 0)
    c2 = lambda b, i: (0, 0)
    return pl.pallas_call(
        functools.partial(_even_kernel, ret_log_decay=_retention_log_decays()),
        grid=(batch, nt),
        in_specs=[
            pl.BlockSpec((ts, d), row),
            pl.BlockSpec((1, d), c2),
            pl.BlockSpec((None,) + w_in_t.shape[1:], lambda b, i: (j, 0, 0),
                         pipeline_mode=pl.Buffered(1)),
            pl.BlockSpec(w_gate_c.shape, c2),
            pl.BlockSpec(w_gate_r.shape, c2),
            pl.BlockSpec(gate_b_c.shape, c2),
            pl.BlockSpec(gate_b_r.shape, c2),
            pl.BlockSpec(qk_conv_w.shape, c2),
            pl.BlockSpec((1, 2 * width), c2),
            pl.BlockSpec((ts, HEAD_DIM), seq_row),
            pl.BlockSpec((ts, HEAD_DIM), seq_row),
            pl.BlockSpec(mlstm_norm_g.shape, c2),
            pl.BlockSpec(ret_norm_g.shape, c2),
            pl.BlockSpec(w_out.shape, c2),
        ],
        out_specs=pl.BlockSpec((ts, d), row),
        out_shape=jax.ShapeDtypeStruct((n_tok, d), F32),
        scratch_shapes=[
            pltpu.VMEM((ts + QK_HALO, 8 * width), F32),
            pltpu.VMEM((ts, 2 * width), F32),
            pltpu.VMEM((ts, 2 * width), BF16),
            pltpu.VMEM((N_PAIRS, PAIR_WIDTH, PAIR_WIDTH), F32),
            pltpu.VMEM((N_PAIRS, 1, PAIR_WIDTH), F32),
            pltpu.VMEM((N_HEADS, 1, HEAD_DIM), F32),
            pltpu.VMEM((N_PAIRS, PAIR_WIDTH, PAIR_WIDTH), F32),
            pltpu.VMEM((d, 8 * width), BF16),
        ],
        compiler_params=pltpu.CompilerParams(
            dimension_semantics=("arbitrary", "arbitrary"), vmem_limit_bytes=VMEM_LIMIT_BYTES),
        name="even_mixer",
    )(x2, norm_g.reshape(1, d), w_in_t, w_gate_c, w_gate_r, gate_b_c, gate_b_r,
      qk_conv_w, qk_conv_b.reshape(1, 2 * width), jnp.asarray(cos_t), jnp.asarray(sin_t),
      mlstm_norm_g, ret_norm_g, w_out.astype(BF16))


def kernel(x, norm_mix_g, norm_ffn_g, final_norm_g, ev_w_in, ev_qk_conv_w, ev_qk_conv_b, ev_i_bias, ev_f_bias, ev_mlstm_norm_g, ev_ret_norm_g, ev_w_out, od_w_in, od_pool_w, od_pool_scale, od_dw_w, od_dw_b, od_conv_norm_g, od_conv_norm_b, od_w_out, ffn_w_gate, ffn_w_up, ffn_w_down):
    batch, seq, d = x.shape
    depth = norm_mix_g.shape[0]
    x2 = x.reshape(batch * seq, d)
    w_gate, w_up, w_down = ffn_w_gate, ffn_w_up, ffn_w_down
    ev_w_in_t = jnp.swapaxes(ev_w_in, 1, 2)
    for layer in range(depth):
        j = layer // 2
        last = layer == depth - 1
        if layer % 2 == 0:
            x2 = _even_mixer(x2, batch, seq, norm_mix_g[layer], ev_w_in_t, j,
                             ev_qk_conv_w[j], ev_qk_conv_b[j], ev_i_bias[j], ev_f_bias[j],
                             ev_mlstm_norm_g[j], ev_ret_norm_g[j], ev_w_out[j])
            x2 = _ffn(x2, layer, norm_ffn_g[layer], w_gate, w_up, w_down, final_norm_g,
                      apply_final_norm=last)
        else:
            x2 = _odd_layer(x2, batch, seq, norm_mix_g[layer], od_w_in[j], od_pool_w[j],
                            od_pool_scale[j], od_dw_w[j], od_dw_b[j], od_conv_norm_g[j],
                            od_conv_norm_b[j], od_w_out[j], norm_ffn_g[layer], layer, w_gate,
                            w_up, w_down, final_norm_g, apply_final_norm=last)
    return x2.reshape(batch, seq, d)
```
